```python
import jax, jax.numpy as jnp
from jax import lax
import numpy as np

D_MODEL = 2048
BATCH = 4
SEQ = 8192
DEPTH = 4

HEAD_DIM = 128
N_MIX_HEADS = D_MODEL // HEAD_DIM
N_ATTN_HEADS = 3 * N_MIX_HEADS // 4
N_SGU_GROUPS = N_MIX_HEADS - N_ATTN_HEADS
SGU_GROUP = HEAD_DIM
ATTN_WIDTH = N_ATTN_HEADS * HEAD_DIM
SGU_WIDTH = N_SGU_GROUPS * SGU_GROUP
MIX_IN = 3 * ATTN_WIDTH + 2 * SGU_WIDTH
CHUNK = 128
DILATED_BRANCHES = ((128, 1), (512, 4), (2048, 16))
BRANCH_BLOCK = 128
ROPE_THETA = 500000.0
ROPE_DIM = HEAD_DIM // 4
CONV_WIDTH = 3
D_FF = -(-8 * D_MODEL // (3 * 256)) * 256
N_EVEN = (DEPTH + 1) // 2
N_ODD = DEPTH // 2
EPS = 1e-6

kernel_name = "hybrid_dilated_attn_sgu_shortconv_adaln"


def rmsnorm(x, g):
    xf = x.astype(jnp.float32)
    y = xf * lax.rsqrt(jnp.mean(xf * xf, axis=-1, keepdims=True) + EPS)
    return (y * g.astype(jnp.float32)).astype(x.dtype)


def rope_tables(positions):
    inv_freq = ROPE_THETA ** (-jnp.arange(0, ROPE_DIM, 2, dtype=jnp.float32) / ROPE_DIM)
    ang = positions.astype(jnp.float32)[..., None] * inv_freq
    return jnp.cos(ang)[:, :, None, :], jnp.sin(ang)[:, :, None, :]


def apply_partial_rope(t, cos, sin):
    half = ROPE_DIM // 2
    t1 = t[..., :half].astype(jnp.float32)
    t2 = t[..., half:ROPE_DIM].astype(jnp.float32)
    rot = jnp.concatenate([t1 * cos - t2 * sin, t2 * cos + t1 * sin], axis=-1).astype(t.dtype)
    return jnp.concatenate([rot, t[..., ROPE_DIM:]], axis=-1)


def _dilated_branch(q, k, v, span, dilation):
    b, s, h, d = q.shape
    n = s // dilation
    nb = -(-n // BRANCH_BLOCK)
    pad = nb * BRANCH_BLOCK - n

    def to_sub(t):
        t = t.reshape(b, n, dilation, h, d).transpose(0, 2, 1, 3, 4)
        t = jnp.pad(t, ((0, 0), (0, 0), (0, pad), (0, 0), (0, 0)))
        return t.reshape(b, dilation, nb, BRANCH_BLOCK, h, d)

    def with_prev(t):
        prev = jnp.pad(t, ((0, 0), (0, 0), (1, 0), (0, 0), (0, 0), (0, 0)))[:, :, :nb]
        return jnp.concatenate([prev, t], axis=3)

    qs = to_sub(q)
    kb = with_prev(to_sub(k))
    vb = with_prev(to_sub(v))
    scores = jnp.einsum("brnqhd,brnkhd->brnqhk", qs, kb).astype(jnp.float32) * (d ** -0.5)

    qi = jnp.arange(BRANCH_BLOCK)[:, None]
    kj = jnp.arange(2 * BRANCH_BLOCK)[None, :]
    dist = qi + BRANCH_BLOCK - kj
    band = (dist >= 0) & (dist <= span)
    not_first = jnp.arange(nb)[:, None, None] > 0
    valid = band[None] & (not_first | (kj >= BRANCH_BLOCK)[None])
    scores = jnp.where(valid[None, None, :, :, None, :], scores, -jnp.inf)

    m = jnp.max(scores, axis=-1, keepdims=True)
    p = jnp.exp(scores - m)
    l = jnp.sum(p, axis=-1)
    o = jnp.einsum("brnqhk,brnkhd->brnqhd", p, vb.astype(jnp.float32)) / l[..., None]
    lse = m[..., 0] + jnp.log(l)

    def from_sub(t):
        t = t.reshape(b, dilation, nb * BRANCH_BLOCK, *t.shape[4:])[:, :, :n]
        t = jnp.moveaxis(t, 1, 2)
        return t.reshape(b, s, *t.shape[3:])

    return from_sub(o), from_sub(lse)


def dilated_attention(q, k, v):
    outs, lses = zip(*[_dilated_branch(q, k, v, w // dil, dil) for w, dil in DILATED_BRANCHES])
    alpha = jax.nn.softmax(jnp.stack(lses, axis=-1), axis=-1)
    o = sum(alpha[..., i, None] * outs[i] for i in range(len(outs)))
    return o.astype(q.dtype)


def spatial_gating(u, v, w_s, b_s):
    b, s, _ = u.shape
    shp = (b, s // CHUNK, CHUNK, N_SGU_GROUPS, SGU_GROUP)
    u = jax.nn.gelu(u).reshape(shp)
    v = jax.nn.gelu(v).reshape(shp)
    w = w_s * jnp.tril(jnp.ones((CHUNK, CHUNK), w_s.dtype))
    mixed = jnp.einsum("gts,bnsgc->bntgc", w, v) + b_s.T[:, :, None]
    return (u * mixed).reshape(b, s, SGU_WIDTH)


def attn_sgu_mixer(h, w_in, w_s, b_s, w_out, cos, sin):
    b, s, _ = h.shape
    z = h @ w_in
    q, k, v, u, vg = jnp.split(
        z, [ATTN_WIDTH, 2 * ATTN_WIDTH, 3 * ATTN_WIDTH, 3 * ATTN_WIDTH + SGU_WIDTH], axis=-1)
    hs = (b, s, N_ATTN_HEADS, HEAD_DIM)
    q = apply_partial_rope(q.reshape(hs), cos, sin)
    k = apply_partial_rope(k.reshape(hs), cos, sin)
    attn = dilated_attention(q, k, v.reshape(hs)).reshape(b, s, ATTN_WIDTH)
    sgu = spatial_gating(u, vg, w_s, b_s)
    return jnp.concatenate([attn, sgu], axis=-1) @ w_out


def short_conv_mixer(h, w_in, conv_w, w_out):
    gb, gc, hx = jnp.split(h @ w_in, 3, axis=-1)
    y = gc * hx
    s = y.shape[1]
    y_pad = jnp.pad(y, ((0, 0), (CONV_WIDTH - 1, 0), (0, 0)))
    conv = sum(conv_w[j] * y_pad[:, j:j + s] for j in range(CONV_WIDTH))
    return (gb * conv) @ w_out


def swiglu(h, w_gate, w_up, w_down):
    return (jax.nn.silu(h @ w_gate) * (h @ w_up)) @ w_down


def setup_inputs(seed: int = 0) -> dict:
    key = jax.random.key(seed)
    ks = jax.random.split(key, 20)
    nrm = jax.random.normal
    f32 = jnp.float32
    d = D_MODEL
    x = nrm(ks[0], (BATCH, SEQ, d), f32)
    c = nrm(ks[1], (BATCH, d), f32)
    positions = (jnp.arange(SEQ, dtype=jnp.int32)[None, :]
                 + jax.random.randint(ks[2], (BATCH, 1), 0, 4096, dtype=jnp.int32))
    ada_w = nrm(ks[3], (DEPTH, d, 6 * d), f32) * (0.5 * d ** -0.5)
    ada_b = nrm(ks[4], (DEPTH, 6 * d), f32) * 0.02
    norm_mix = 1.0 + 0.02 * nrm(ks[5], (DEPTH, d), f32)
    norm_ffn = 1.0 + 0.02 * nrm(ks[6], (DEPTH, d), f32)
    ab_w_in = nrm(ks[7], (N_EVEN, d, MIX_IN), f32) * d ** -0.5
    sgu_w = nrm(ks[8], (N_EVEN, N_SGU_GROUPS, CHUNK, CHUNK), f32) * CHUNK ** -0.5
    sgu_b = 1.0 + 0.1 * nrm(ks[9], (N_EVEN, N_SGU_GROUPS, CHUNK), f32)
    ab_w_out = nrm(ks[10], (N_EVEN, ATTN_WIDTH + SGU_WIDTH, d), f32) * (ATTN_WIDTH + SGU_WIDTH) ** -0.5
    conv_w_in = nrm(ks[11], (N_ODD, d, 3 * d), f32) * d ** -0.5
    conv_w = nrm(ks[12], (N_ODD, CONV_WIDTH, d), f32) * CONV_WIDTH ** -0.5
    conv_w_out = nrm(ks[13], (N_ODD, d, d), f32) * d ** -0.5
    ffn_w_gate = nrm(ks[14], (DEPTH, d, D_FF), f32) * d ** -0.5
    ffn_w_up = nrm(ks[15], (DEPTH, d, D_FF), f32) * d ** -0.5
    ffn_w_down = nrm(ks[16], (DEPTH, D_FF, d), f32) * D_FF ** -0.5
    final_norm = 1.0 + 0.02 * nrm(ks[17], (d,), f32)
    return {"x": x, "c": c, "positions": positions, "ada_w": ada_w, "ada_b": ada_b,
            "norm_mix": norm_mix, "norm_ffn": norm_ffn, "ab_w_in": ab_w_in,
            "sgu_w": sgu_w, "sgu_b": sgu_b, "ab_w_out": ab_w_out,
            "conv_w_in": conv_w_in, "conv_w": conv_w, "conv_w_out": conv_w_out,
            "ffn_w_gate": ffn_w_gate, "ffn_w_up": ffn_w_up, "ffn_w_down": ffn_w_down,
            "final_norm": final_norm}


def reference(x, c, positions, ada_w, ada_b, norm_mix, norm_ffn, ab_w_in, sgu_w, sgu_b,
              ab_w_out, conv_w_in, conv_w, conv_w_out, ffn_w_gate, ffn_w_up, ffn_w_down,
              final_norm):
    cos, sin = rope_tables(positions)
    c_act = jax.nn.silu(c)
    for layer in range(DEPTH):
        mod = (c_act @ ada_w[layer] + ada_b[layer])[:, None, :]
        sh_m, sc_m, g_m, sh_f, sc_f, g_f = jnp.split(mod, 6, axis=-1)
        h = rmsnorm(x, norm_mix[layer]) * (1 + sc_m) + sh_m
        i = layer // 2
        if layer % 2 == 0:
            mix = attn_sgu_mixer(h, ab_w_in[i], sgu_w[i], sgu_b[i], ab_w_out[i], cos, sin)
        else:
            mix = short_conv_mixer(h, conv_w_in[i], conv_w[i], conv_w_out[i])
        x = x + g_m * mix
        h = rmsnorm(x, norm_ffn[layer]) * (1 + sc_f) + sh_f
        x = x + g_f * swiglu(h, ffn_w_gate[layer], ffn_w_up[layer], ffn_w_down[layer])
    return rmsnorm(x, final_norm)
```

```python
import functools
import math

import jax
import jax.numpy as jnp
from jax import lax
from jax.experimental import pallas as pl
from jax.experimental.pallas import tpu as pltpu

F32 = jnp.float32
BF16 = jnp.bfloat16

HEAD_DIM = 128
ROPE_DIM = HEAD_DIM // 4
ROPE_HALF = ROPE_DIM // 2
ROPE_THETA = 500000.0
EPS = 1e-6
CHUNK = 128
BLOCK = 128
DILATIONS = (1, 4, 16)
SUPER = BLOCK * DILATIONS[-1]
CONV_WIDTH = 3
HALO = 16
NEG = -1e30
VMEM_LIMIT_BYTES = 56 * 1024 * 1024


def _cparams(sem):
    return pltpu.CompilerParams(dimension_semantics=sem, vmem_limit_bytes=VMEM_LIMIT_BYTES)


def _normmod(x, g, sc, sh):
    y = x * lax.rsqrt(jnp.mean(x * x, axis=-1, keepdims=True) + EPS)
    return (y * g) * (1.0 + sc) + sh


def _mod_kernel(c_ref, w_ref, b_ref, o_ref):
    ca = jax.nn.silu(c_ref[...]).astype(BF16)
    o_ref[...] = jnp.dot(ca, w_ref[...].astype(BF16), preferred_element_type=F32) + b_ref[...]


def _modulation(c, ada_w, ada_b):
    depth, d, n = ada_w.shape
    b = c.shape[0]
    rows = -(-b // 8) * 8
    c_pad = jnp.pad(c, ((0, rows - b), (0, 0)))
    tn = min(n, 1024)
    out = pl.pallas_call(
        _mod_kernel,
        grid=(depth, n // tn),
        in_specs=[
            pl.BlockSpec((rows, d), lambda l, j: (0, 0)),
            pl.BlockSpec((None, d, tn), lambda l, j: (l, 0, j)),
            pl.BlockSpec((None, 1, tn), lambda l, j: (l, 0, j)),
        ],
        out_specs=pl.BlockSpec((None, rows, tn), lambda l, j: (l, 0, j)),
        out_shape=jax.ShapeDtypeStruct((depth, rows, n), F32),
        compiler_params=_cparams(("arbitrary", "arbitrary")),
        name="adaln_mod",
    )(c_pad, ada_w, ada_b.reshape(depth, 1, n))
    return out[:, :b]


def _inproj_kernel(x_ref, g_ref, sc_ref, sh_ref, w_ref, cos_ref, sa_ref, sb_ref, o_ref, h_ref,
                   *, n_q_tiles, n_rope_tiles, heads_per_tile, scale):
    j = pl.program_id(1)

    @pl.when(j == 0)
    def _():
        h_ref[...] = _normmod(x_ref[...], g_ref[...], sc_ref[...], sh_ref[...]).astype(BF16)

    acc = jnp.dot(h_ref[...], w_ref[...], preferred_element_type=F32)

    @pl.when(j < n_rope_tiles)
    def _():
        cs, sa, sb = cos_ref[...], sa_ref[...], sb_ref[...]
        mult = jnp.where(j < n_q_tiles, scale, 1.0).astype(F32)
        for hh in range(heads_per_tile):
            t = acc[:, hh * HEAD_DIM:(hh + 1) * HEAD_DIM]
            r = t * cs + pltpu.roll(t, HEAD_DIM - ROPE_HALF, 1) * sa + pltpu.roll(t, ROPE_HALF, 1) * sb
            o_ref[hh] = (r * mult).astype(BF16)

    @pl.when(j >= n_rope_tiles)
    def _():
        for hh in range(heads_per_tile):
            o_ref[hh] = acc[:, hh * HEAD_DIM:(hh + 1) * HEAD_DIM].astype(BF16)


def _inproj(x, g, sc, sh, w, rope, n_attn, n_sgu, tm):
    b, s, d = x.shape
    n = w.shape[1]
    n_slots = n // HEAD_DIM
    hpt = math.gcd(math.gcd(n_attn, n_sgu), 4)
    tn = hpt * HEAD_DIM
    tiles_per_seq = s // tm
    cosf, sina, sinb = rope
    kern = functools.partial(
        _inproj_kernel, n_q_tiles=n_attn // hpt, n_rope_tiles=2 * n_attn // hpt,
        heads_per_tile=hpt, scale=HEAD_DIM ** -0.5)
    row = lambda i, j: (i // tiles_per_seq, i % tiles_per_seq, 0)
    per_b = lambda i, j: (i // tiles_per_seq, 0, 0)
    return pl.pallas_call(
        kern,
        grid=(b * tiles_per_seq, n_slots // hpt),
        in_specs=[
            pl.BlockSpec((None, tm, d), row),
            pl.BlockSpec((1, d), lambda i, j: (0, 0)),
            pl.BlockSpec((None, 1, d), per_b),
            pl.BlockSpec((None, 1, d), per_b),
            pl.BlockSpec((d, tn), lambda i, j: (0, j)),
            pl.BlockSpec((None, tm, HEAD_DIM), row),
            pl.BlockSpec((None, tm, HEAD_DIM), row),
            pl.BlockSpec((None, tm, HEAD_DIM), row),
        ],
        out_specs=pl.BlockSpec((None, hpt, tm, HEAD_DIM),
                               lambda i, j: (i // tiles_per_seq, j, i % tiles_per_seq, 0)),
        out_shape=jax.ShapeDtypeStruct((b, n_slots, s, HEAD_DIM), BF16),
        scratch_shapes=[pltpu.VMEM((tm, d), BF16)],
        compiler_params=_cparams(("arbitrary", "arbitrary")),
        name="even_inproj",
    )(x, g, sc, sh, w, cosf, sina, sinb)


def _attn_block(q, k, v, bias):
    s = lax.dot_general(q, k, (((1,), (1,)), ((), ())), preferred_element_type=F32) + bias
    m = jnp.max(s, axis=1, keepdims=True)
    p = jnp.exp(s - m)
    l = jnp.sum(p, axis=1, keepdims=True)
    acc = jnp.dot(p.astype(BF16), v, preferred_element_type=F32)
    return acc, m, l


def _merge(acc_a, m_a, l_a, acc_b, m_b, l_b):
    m = jnp.maximum(m_a, m_b)
    ea = jnp.exp(m_a - m)
    eb = jnp.exp(m_b - m)
    return acc_a * ea + acc_b * eb, m, l_a * ea + l_b * eb


def _attn_kernel(q1_ref, q4_ref, q16_ref,
                 k1_ref, k1p_ref, k4_ref, k4p_ref, k16_ref, k16p_ref,
                 v1_ref, v1p_ref, v4_ref, v4p_ref, v16_ref, v16p_ref,
                 o_ref,
                 a16_ref, m16_ref, l16_ref, a4_ref, m4_ref, l4_ref):
    n = pl.program_id(2)
    d4, d16 = DILATIONS[1], DILATIONS[2]
    sub4 = SUPER // d4
    ratio = d16 // d4
    shape = (BLOCK, HEAD_DIM)

    qi = lax.broadcasted_iota(jnp.int32, (BLOCK, 2 * BLOCK), 0)
    kj = lax.broadcasted_iota(jnp.int32, (BLOCK, 2 * BLOCK), 1)
    band = (kj >= qi) & (kj <= qi + BLOCK)
    bias_band = jnp.where(band, 0.0, NEG).astype(F32)
    bias_head = jnp.where(n == 0, jnp.where(band & (kj >= BLOCK), 0.0, NEG).astype(F32), bias_band)

    def lanes(r):
        return slice(r * HEAD_DIM, (r + 1) * HEAD_DIM)

    for r in range(d16):
        a, bb = r // d4, r % d4
        k = jnp.concatenate([k16p_ref[:, lanes(r)], k16_ref[:, lanes(r)]], axis=0)
        v = jnp.concatenate([v16p_ref[:, lanes(r)], v16_ref[:, lanes(r)]], axis=0)
        acc, m, l = _attn_block(q16_ref[:, lanes(r)], k, v, bias_head)
        rows = pl.ds(bb * sub4 + a, BLOCK, stride=ratio)
        a16_ref[rows, :] = acc
        m16_ref[rows, :] = jnp.broadcast_to(m, shape)
        l16_ref[rows, :] = jnp.broadcast_to(l, shape)

    def branch4(r, jj, k, v, bias):
        q = q4_ref[pl.ds(jj * BLOCK, BLOCK), lanes(r)]
        acc, m, l = _attn_block(q, k, v, bias)
        src = pl.ds(r * sub4 + jj * BLOCK, BLOCK)
        acc, m, l = _merge(acc, m, l, a16_ref[src, :], m16_ref[src, :], l16_ref[src, :])
        rows = pl.ds(jj * (BLOCK * d4) + r, BLOCK, stride=d4)
        a4_ref[rows, :] = acc
        m4_ref[rows, :] = m
        l4_ref[rows, :] = l

    for r in range(d4):
        k = jnp.concatenate([k4p_ref[:, lanes(r)], k4_ref[pl.ds(0, BLOCK), lanes(r)]], axis=0)
        v = jnp.concatenate([v4p_ref[:, lanes(r)], v4_ref[pl.ds(0, BLOCK), lanes(r)]], axis=0)
        branch4(r, 0, k, v, bias_head)
        for jj in range(1, sub4 // BLOCK):
            keys = pl.ds((jj - 1) * BLOCK, 2 * BLOCK)
            branch4(r, jj, k4_ref[keys, lanes(r)], v4_ref[keys, lanes(r)], bias_band)

    def branch1(ii, k, v, bias):
        acc, m, l = _attn_block(q1_ref[pl.ds(ii * BLOCK, BLOCK), :], k, v, bias)
        src = pl.ds(ii * BLOCK, BLOCK)
        acc, m, l = _merge(acc, m, l, a4_ref[src, :], m4_ref[src, :], l4_ref[src, :])
        o_ref[src, :] = (acc / l).astype(o_ref.dtype)

    k = jnp.concatenate([k1p_ref[...], k1_ref[pl.ds(0, BLOCK), :]], axis=0)
    v = jnp.concatenate([v1p_ref[...], v1_ref[pl.ds(0, BLOCK), :]], axis=0)
    branch1(0, k, v, bias_head)

    def body(ii, carry):
        keys = pl.ds(pl.multiple_of((ii - 1) * BLOCK, BLOCK), 2 * BLOCK)
        branch1_dyn(ii, k1_ref[keys, :], v1_ref[keys, :])
        return carry

    def branch1_dyn(ii, k, v):
        src = pl.ds(pl.multiple_of(ii * BLOCK, BLOCK), BLOCK)
        acc, m, l = _attn_block(q1_ref[src, :], k, v, bias_band)
        acc, m, l = _merge(acc, m, l, a4_ref[src, :], m4_ref[src, :], l4_ref[src, :])
        o_ref[src, :] = (acc / l).astype(o_ref.dtype)

    lax.fori_loop(1, SUPER // BLOCK, body, 0)


def _attention(zh, n_attn):
    b, n_slots, s, hd = zh.shape
    d4, d16 = DILATIONS[1], DILATIONS[2]
    views = {d: zh.reshape(b, n_slots, s // d, d * hd) for d in DILATIONS}
    n_super = s // SUPER

    def cur(d, slot0):
        return pl.BlockSpec((None, None, SUPER // d, d * hd),
                            lambda bi, h, n: (bi, slot0 + h, n, 0))

    def prev(d, slot0):
        per = SUPER // d // BLOCK
        return pl.BlockSpec((None, None, BLOCK, d * hd),
                            lambda bi, h, n: (bi, slot0 + h, jnp.maximum(n * per - 1, 0), 0))

    in_specs = [cur(1, 0), cur(d4, 0), cur(d16, 0)]
    args = [views[1], views[d4], views[d16]]
    for slot0 in (n_attn, 2 * n_attn):
        for d in DILATIONS:
            in_specs += [cur(d, slot0), prev(d, slot0)]
            args += [views[d], views[d]]
    scratch = [pltpu.VMEM((SUPER, hd), F32) for _ in range(6)]
    return pl.pallas_call(
        _attn_kernel,
        grid=(b, n_attn, n_super),
        in_specs=in_specs,
        out_specs=pl.BlockSpec((None, SUPER, hd), lambda bi, h, n: (bi, n, h)),
        out_shape=jax.ShapeDtypeStruct((b, s, n_attn * hd), BF16),
        scratch_shapes=scratch,
        compiler_params=_cparams(("arbitrary", "arbitrary", "arbitrary")),
        name="dilated_attention",
    )(*args)


def _sgu_kernel(u_ref, v_ref, w_ref, b_ref, o_ref, *, n_chunks):
    row = lax.broadcasted_iota(jnp.int32, (CHUNK, CHUNK), 0)
    col = lax.broadcasted_iota(jnp.int32, (CHUNK, CHUNK), 1)
    w = (w_ref[...] * (row >= col).astype(F32)).astype(BF16)
    bias = b_ref[...]
    for c in range(n_chunks):
        rows = pl.ds(c * CHUNK, CHUNK)
        v = jax.nn.gelu(v_ref[rows, :].astype(F32))
        u = jax.nn.gelu(u_ref[rows, :].astype(F32))
        mixed = jnp.dot(w, v.astype(BF16), preferred_element_type=F32) + bias
        o_ref[rows, :] = (u * mixed).astype(o_ref.dtype)


def _sgu(zh, w_s, b_s, n_attn, n_sgu, rows):
    b, _, s, hd = zh.shape
    u0, v0 = 3 * n_attn, 3 * n_attn + n_sgu
    kern = functools.partial(_sgu_kernel, n_chunks=rows // CHUNK)
    return pl.pallas_call(
        kern,
        grid=(b, n_sgu, s // rows),
        in_specs=[
            pl.BlockSpec((None, None, rows, hd), lambda bi, g, n: (bi, u0 + g, n, 0)),
            pl.BlockSpec((None, None, rows, hd), lambda bi, g, n: (bi, v0 + g, n, 0)),
            pl.BlockSpec((None, CHUNK, CHUNK), lambda bi, g, n: (g, 0, 0)),
            pl.BlockSpec((None, CHUNK, 1), lambda bi, g, n: (g, 0, 0)),
        ],
        out_specs=pl.BlockSpec((None, rows, hd), lambda bi, g, n: (bi, n, g)),
        out_shape=jax.ShapeDtypeStruct((b, s, n_sgu * hd), BF16),
        compiler_params=_cparams(("arbitrary", "arbitrary", "arbitrary")),
        name="spatial_gating",
    )(zh, zh, w_s, b_s.reshape(n_sgu, CHUNK, 1))


def _outproj_kernel(x_ref, a_ref, s_ref, wa_ref, ws_ref, gate_ref, o_ref):
    acc = jnp.dot(a_ref[...], wa_ref[...], preferred_element_type=F32)
    acc += jnp.dot(s_ref[...], ws_ref[...], preferred_element_type=F32)
    o_ref[...] = x_ref[...] + gate_ref[...] * acc


def _outproj(x, attn, sgu, w_attn, w_sgu, gate, tm):
    b, s, d = x.shape
    wa, ws = attn.shape[-1], sgu.shape[-1]
    tiles_per_seq = s // tm
    row = lambda i: (i // tiles_per_seq, i % tiles_per_seq, 0)
    return pl.pallas_call(
        _outproj_kernel,
        grid=(b * tiles_per_seq,),
        in_specs=[
            pl.BlockSpec((None, tm, d), row),
            pl.BlockSpec((None, tm, wa), row),
            pl.BlockSpec((None, tm, ws), row),
            pl.BlockSpec((wa, d), lambda i: (0, 0)),
            pl.BlockSpec((ws, d), lambda i: (0, 0)),
            pl.BlockSpec((None, 1, d), lambda i: (i // tiles_per_seq, 0, 0)),
        ],
        out_specs=pl.BlockSpec((None, tm, d), row),
        out_shape=jax.ShapeDtypeStruct((b, s, d), F32),
        compiler_params=_cparams(("arbitrary",)),
        name="even_outproj",
    )(x, attn, sgu, w_attn, w_sgu, gate)


def _conv_kernel(x_ref, xh_ref, g_ref, sc_ref, sh_ref, gate_ref, wb_ref, wc_ref, wx_ref, cw_ref,
                 wo_ref, o_ref, h_ref, *, tm, tiles_per_seq):
    i, j = pl.program_id(0), pl.program_id(1)

    @pl.when(j == 0)
    def _():
        g, sc, sh = g_ref[...], sc_ref[...], sh_ref[...]
        h_ref[pl.ds(HALO, tm), :] = _normmod(x_ref[...], g, sc, sh).astype(BF16)
        halo = _normmod(xh_ref[...], g, sc, sh)
        h_ref[pl.ds(0, HALO), :] = jnp.where(i % tiles_per_seq == 0, 0.0, halo).astype(BF16)
        o_ref[...] = jnp.zeros_like(o_ref)

    h_ext = h_ref[...]
    y = (jnp.dot(h_ext, wc_ref[...], preferred_element_type=F32)
         * jnp.dot(h_ext, wx_ref[...], preferred_element_type=F32))
    cw = cw_ref[...]
    conv = cw[2:3] * y + cw[1:2] * pltpu.roll(y, 1, 0) + cw[0:1] * pltpu.roll(y, 2, 0)
    gb = jnp.dot(h_ref[pl.ds(HALO, tm), :], wb_ref[...], preferred_element_type=F32)
    a = (gb * conv[HALO:]).astype(BF16)
    o_ref[...] += jnp.dot(a, wo_ref[...], preferred_element_type=F32)

    @pl.when(j == pl.num_programs(1) - 1)
    def _():
        o_ref[...] = x_ref[...] + gate_ref[...] * o_ref[...]


def _conv_mixer(x, g, sc, sh, gate, w_in, conv_w, w_out, tm, tc):
    b, s, d = x.shape
    tiles_per_seq = s // tm
    nt = d // tc
    row = lambda i, j: (i // tiles_per_seq, i % tiles_per_seq, 0)
    per_b = lambda i, j: (i // tiles_per_seq, 0, 0)
    halo_blocks = tm // HALO
    halo = lambda i, j: (i // tiles_per_seq,
                         jnp.maximum((i % tiles_per_seq) * halo_blocks - 1, 0), 0)
    kern = functools.partial(_conv_kernel, tm=tm, tiles_per_seq=tiles_per_seq)
    return pl.pallas_call(
        kern,
        grid=(b * tiles_per_seq, nt),
        in_specs=[
            pl.BlockSpec((None, tm, d), row),
            pl.BlockSpec((None, HALO, d), halo),
            pl.BlockSpec((1, d), lambda i, j: (0, 0)),
            pl.BlockSpec((None, 1, d), per_b),
            pl.BlockSpec((None, 1, d), per_b),
            pl.BlockSpec((None, 1, d), per_b),
            pl.BlockSpec((d, tc), lambda i, j: (0, j)),
            pl.BlockSpec((d, tc), lambda i, j: (0, nt + j)),
            pl.BlockSpec((d, tc), lambda i, j: (0, 2 * nt + j)),
            pl.BlockSpec((CONV_WIDTH, tc), lambda i, j: (0, j)),
            pl.BlockSpec((tc, d), lambda i, j: (j, 0)),
        ],
        out_specs=pl.BlockSpec((None, tm, d), row),
        out_shape=jax.ShapeDtypeStruct((b, s, d), F32),
        scratch_shapes=[pltpu.VMEM((HALO + tm, d), BF16)],
        compiler_params=_cparams(("arbitrary", "arbitrary")),
        name="conv_mixer",
    )(x, x, g, sc, sh, gate, w_in, w_in, w_in, conv_w, w_out)


def _ffn_kernel(x_ref, g_ref, sc_ref, sh_ref, gate_ref, wg_ref, wu_ref, wd_ref, fn_ref, o_ref, h_ref,
                *, final):
    f = pl.program_id(1)

    @pl.when(f == 0)
    def _():
        h_ref[...] = _normmod(x_ref[...], g_ref[...], sc_ref[...], sh_ref[...]).astype(BF16)
        o_ref[...] = jnp.zeros_like(o_ref)

    h = h_ref[...]
    gt = jnp.dot(h, wg_ref[...], preferred_element_type=F32)
    up = jnp.dot(h, wu_ref[...], preferred_element_type=F32)
    a = (jax.nn.silu(gt) * up).astype(BF16)
    o_ref[...] += jnp.dot(a, wd_ref[...], preferred_element_type=F32)

    @pl.when(f == pl.num_programs(1) - 1)
    def _():
        y = x_ref[...] + gate_ref[...] * o_ref[...]
        if final:
            y = (y * lax.rsqrt(jnp.mean(y * y, axis=-1, keepdims=True) + EPS)) * fn_ref[...]
        o_ref[...] = y


def _ffn(x, g, sc, sh, gate, w_gate, w_up, w_down, final_gain, final, tm, tf):
    b, s, d = x.shape
    dff = w_gate.shape[1]
    tiles_per_seq = s // tm
    row = lambda i, f: (i // tiles_per_seq, i % tiles_per_seq, 0)
    per_b = lambda i, f: (i // tiles_per_seq, 0, 0)
    const = lambda i, f: (0, 0)
    kern = functools.partial(_ffn_kernel, final=final)
    return pl.pallas_call(
        kern,
        grid=(b * tiles_per_seq, dff // tf),
        in_specs=[
            pl.BlockSpec((None, tm, d), row),
            pl.BlockSpec((1, d), const),
            pl.BlockSpec((None, 1, d), per_b),
            pl.BlockSpec((None, 1, d), per_b),
            pl.BlockSpec((None, 1, d), per_b),
            pl.BlockSpec((d, tf), lambda i, f: (0, f)),
            pl.BlockSpec((d, tf), lambda i, f: (0, f)),
            pl.BlockSpec((tf, d), lambda i, f: (f, 0)),
            pl.BlockSpec((1, d), const),
        ],
        out_specs=pl.BlockSpec((None, tm, d), row),
        out_shape=jax.ShapeDtypeStruct((b, s, d), F32),
        scratch_shapes=[pltpu.VMEM((tm, d), BF16)],
        compiler_params=_cparams(("arbitrary", "arbitrary")),
        name="swiglu_ffn_final" if final else "swiglu_ffn",
    )(x, g, sc, sh, gate, w_gate, w_up, w_down, final_gain)


def _rope_tables(positions):
    inv_freq = ROPE_THETA ** (-jnp.arange(0, ROPE_DIM, 2, dtype=F32) / ROPE_DIM)
    ang = positions.astype(F32)[..., None] * inv_freq
    cos, sin = jnp.cos(ang), jnp.sin(ang)
    rest = HEAD_DIM - ROPE_DIM
    ones = jnp.ones(ang.shape[:-1] + (rest,), F32)
    zeros = jnp.zeros(ang.shape[:-1] + (rest,), F32)
    z_half = jnp.zeros_like(sin)
    cosf = jnp.concatenate([cos, cos, ones], axis=-1)
    sina = jnp.concatenate([-sin, z_half, zeros], axis=-1)
    sinb = jnp.concatenate([z_half, sin, zeros], axis=-1)
    return cosf, sina, sinb


def _pick(n, candidates):
    for c in candidates:
        if n % c == 0:
            return c
    return n


def kernel(x, c, positions, ada_w, ada_b, norm_mix, norm_ffn, ab_w_in, sgu_w, sgu_b, ab_w_out,
           conv_w_in, conv_w, conv_w_out, ffn_w_gate, ffn_w_up, ffn_w_down, final_norm):
    b, s, d = x.shape
    depth = ada_w.shape[0]
    n_slots = d // HEAD_DIM
    n_attn = 3 * n_slots // 4
    n_sgu = n_slots - n_attn
    attn_w = n_attn * HEAD_DIM
    dff = ffn_w_gate.shape[-1]
    assert s % SUPER == 0 and d % HEAD_DIM == 0

    tm_big = _pick(s, (1024, 512, 256, 128))
    tm_small = _pick(s, (512, 256, 128))
    tf = _pick(dff, (512, 256, 128))
    tc = _pick(d, (512, 256, 128))

    mod = _modulation(c, ada_w, ada_b)
    mod = mod.reshape(depth, b, 6, 1, d)
    rope = _rope_tables(positions)
    final_gain = final_norm.reshape(1, d)

    for layer in range(depth):
        sh_m, sc_m, g_m, sh_f, sc_f, g_f = (mod[layer, :, k] for k in range(6))
        gain_m = norm_mix[layer].reshape(1, d)
        gain_f = norm_ffn[layer].reshape(1, d)
        i = layer // 2
        if layer % 2 == 0:
            zh = _inproj(x, gain_m, sc_m, sh_m, ab_w_in[i].astype(BF16), rope, n_attn, n_sgu, tm_big)
            attn = _attention(zh, n_attn)
            sgu = _sgu(zh, sgu_w[i], sgu_b[i], n_attn, n_sgu, _pick(s, (2048, 1024, 512, 256, 128)))
            w_out = ab_w_out[i].astype(BF16)
            x = _outproj(x, attn, sgu, w_out[:attn_w], w_out[attn_w:], g_m, tm_small)
        else:
            x = _conv_mixer(x, gain_m, sc_m, sh_m, g_m, conv_w_in[i].astype(BF16), conv_w[i],
                            conv_w_out[i].astype(BF16), tm_small, tc)
        x = _ffn(x, gain_f, sc_f, sh_f, g_f, ffn_w_gate[layer].astype(BF16),
                 ffn_w_up[layer].astype(BF16), ffn_w_down[layer].astype(BF16), final_gain,
                 layer == depth - 1, tm_small, tf)
    return x
```

```python
import functools
import math

import jax
import jax.numpy as jnp
from jax import lax
from jax.experimental import pallas as pl
from jax.experimental.pallas import tpu as pltpu

F32 = jnp.float32
BF16 = jnp.bfloat16

HEAD_DIM = 128
ROPE_DIM = HEAD_DIM // 4
ROPE_HALF = ROPE_DIM // 2
ROPE_THETA = 500000.0
EPS = 1e-6
CHUNK = 128
BLOCK = 128
DILATIONS = (1, 4, 16)
SUPER = BLOCK * DILATIONS[-1]
CONV_WIDTH = 3
HALO = 16
NEG = -1e30
VMEM_LIMIT_BYTES = 56 * 1024 * 1024


def _cparams(sem):
    return pltpu.CompilerParams(dimension_semantics=sem, vmem_limit_bytes=VMEM_LIMIT_BYTES)


def _normmod(x, g, sc, sh):
    y = x * lax.rsqrt(jnp.mean(x * x, axis=-1, keepdims=True) + EPS)
    return (y * g) * (1.0 + sc) + sh


NORM_ROWS = 32


def _normmod_rows(h_ref, x_ref, g_ref, sc_ref, sh_ref, rows, offset=0):
    g, sc, sh = g_ref[...], sc_ref[...], sh_ref[...]

    def body(c, carry):
        r0 = pl.multiple_of(c * NORM_ROWS, NORM_ROWS)
        h = _normmod(x_ref[pl.ds(r0, NORM_ROWS), :], g, sc, sh)
        h_ref[pl.ds(offset + r0, NORM_ROWS), :] = h.astype(h_ref.dtype)
        return carry

    lax.fori_loop(0, rows // NORM_ROWS, body, 0, unroll=2)


def _mod_kernel(c_ref, w_ref, b_ref, o_ref):
    ca = jax.nn.silu(c_ref[...]).astype(BF16)
    o_ref[...] = jnp.dot(ca, w_ref[...].astype(BF16), preferred_element_type=F32) + b_ref[...]


def _modulation(c, ada_w, ada_b):
    depth, d, n = ada_w.shape
    b = c.shape[0]
    rows = -(-b // 8) * 8
    c_pad = jnp.pad(c, ((0, rows - b), (0, 0)))
    tn = min(n, 1024)
    out = pl.pallas_call(
        _mod_kernel,
        grid=(depth, n // tn),
        in_specs=[
            pl.BlockSpec((rows, d), lambda l, j: (0, 0)),
            pl.BlockSpec((None, d, tn), lambda l, j: (l, 0, j)),
            pl.BlockSpec((None, 1, tn), lambda l, j: (l, 0, j)),
        ],
        out_specs=pl.BlockSpec((None, rows, tn), lambda l, j: (l, 0, j)),
        out_shape=jax.ShapeDtypeStruct((depth, rows, n), F32),
        compiler_params=_cparams(("arbitrary", "arbitrary")),
        name="adaln_mod",
    )(c_pad, ada_w, ada_b.reshape(depth, 1, n))
    return out[:, :b]


def _inproj_kernel(x_ref, g_ref, sc_ref, sh_ref, w_ref, cos_ref, sa_ref, sb_ref,
                   o1_ref, o4_ref, o16_ref, h_ref, s1_ref, s4_ref,
                   *, n_q_tiles, n_rope_tiles, n_qkv_tiles, heads_per_tile, scale, tm):
    j = pl.program_id(1)
    d4, d16 = DILATIONS[1], DILATIONS[2]
    ratio = d16 // d4

    @pl.when(j == 0)
    def _():
        _normmod_rows(h_ref, x_ref, g_ref, sc_ref, sh_ref, tm)

    acc = jnp.dot(h_ref[...], w_ref[...], preferred_element_type=F32)

    def lanes(r):
        return slice(r * HEAD_DIM, (r + 1) * HEAD_DIM)

    @pl.when(j < n_rope_tiles)
    def _():
        cs, sa, sb = cos_ref[...], sa_ref[...], sb_ref[...]
        mult = jnp.where(j < n_q_tiles, scale, 1.0).astype(F32)
        for hh in range(heads_per_tile):
            t = acc[:, lanes(hh)]
            r = t * cs + pltpu.roll(t, HEAD_DIM - ROPE_HALF, 1) * sa + pltpu.roll(t, ROPE_HALF, 1) * sb
            s1_ref[hh] = r * mult

    @pl.when((j >= n_rope_tiles) & (j < n_qkv_tiles))
    def _():
        for hh in range(heads_per_tile):
            s1_ref[hh] = acc[:, lanes(hh)]

    @pl.when(j < n_qkv_tiles)
    def _():
        for hh in range(heads_per_tile):
            o1_ref[hh] = s1_ref[hh].astype(BF16)
            for r in range(d4):
                part = s1_ref[hh, pl.ds(r, tm // d4, stride=d4), :]
                o4_ref[hh, :, lanes(r)] = part.astype(BF16)
                s4_ref[hh, pl.ds(r * (tm // d4), tm // d4), :] = part
            for r in range(d16):
                a, bb = r // d4, r % d4
                part = s4_ref[hh, pl.ds(bb * (tm // d4) + a, tm // d16, stride=ratio), :]
                o16_ref[hh, :, lanes(r)] = part.astype(BF16)

    @pl.when(j >= n_qkv_tiles)
    def _():
        for hh in range(heads_per_tile):
            o1_ref[hh] = acc[:, lanes(hh)].astype(BF16)


def _inproj(x, g, sc, sh, w, li, rope, n_attn, n_sgu, tm):
    b, s, d = x.shape
    n = w.shape[-1]
    n_slots = n // HEAD_DIM
    hpt = math.gcd(math.gcd(n_attn, n_sgu), 4)
    tn = hpt * HEAD_DIM
    tiles_per_seq = s // tm
    d4, d16 = DILATIONS[1], DILATIONS[2]
    n_qkv_tiles = 3 * n_attn // hpt
    cosf, sina, sinb = rope
    kern = functools.partial(
        _inproj_kernel, n_q_tiles=n_attn // hpt, n_rope_tiles=2 * n_attn // hpt,
        n_qkv_tiles=n_qkv_tiles, heads_per_tile=hpt, scale=HEAD_DIM ** -0.5, tm=tm)
    row = lambda i, j: (i // tiles_per_seq, i % tiles_per_seq, 0)
    per_b = lambda i, j: (i // tiles_per_seq, 0, 0)
    qkv_out = lambda i, j: (i // tiles_per_seq, jnp.minimum(j, n_qkv_tiles - 1), i % tiles_per_seq, 0)
    return pl.pallas_call(
        kern,
        grid=(b * tiles_per_seq, n_slots // hpt),
        in_specs=[
            pl.BlockSpec((None, tm, d), row),
            pl.BlockSpec((1, d), lambda i, j: (0, 0)),
            pl.BlockSpec((None, 1, d), per_b),
            pl.BlockSpec((None, 1, d), per_b),
            pl.BlockSpec((None, d, tn), lambda i, j: (li, 0, j)),
            pl.BlockSpec((None, tm, HEAD_DIM), row),
            pl.BlockSpec((None, tm, HEAD_DIM), row),
            pl.BlockSpec((None, tm, HEAD_DIM), row),
        ],
        out_specs=[
            pl.BlockSpec((None, hpt, tm, HEAD_DIM),
                         lambda i, j: (i // tiles_per_seq, j, i % tiles_per_seq, 0)),
            pl.BlockSpec((None, hpt, tm // d4, d4 * HEAD_DIM), qkv_out),
            pl.BlockSpec((None, hpt, tm // d16, d16 * HEAD_DIM), qkv_out),
        ],
        out_shape=[
            jax.ShapeDtypeStruct((b, n_slots, s, HEAD_DIM), BF16),
            jax.ShapeDtypeStruct((b, 3 * n_attn, s // d4, d4 * HEAD_DIM), BF16),
            jax.ShapeDtypeStruct((b, 3 * n_attn, s // d16, d16 * HEAD_DIM), BF16),
        ],
        scratch_shapes=[
            pltpu.VMEM((tm, d), BF16),
            pltpu.VMEM((hpt, tm, HEAD_DIM), F32),
            pltpu.VMEM((hpt, tm, HEAD_DIM), F32),
        ],
        compiler_params=_cparams(("arbitrary", "arbitrary")),
        name="even_inproj",
    )(x, g, sc, sh, w, cosf, sina, sinb)


def _attn_block(q, k, v, bias):
    s = lax.dot_general(q, k, (((1,), (1,)), ((), ())), preferred_element_type=F32) + bias
    m = jnp.max(s, axis=1, keepdims=True)
    p = jnp.exp(s - m)
    l = jnp.sum(p, axis=1, keepdims=True)
    acc = jnp.dot(p.astype(BF16), v, preferred_element_type=F32)
    return acc, m, l


def _merge(acc_a, m_a, l_a, acc_b, m_b, l_b):
    m = jnp.maximum(m_a, m_b)
    ea = jnp.exp(m_a - m)
    eb = jnp.exp(m_b - m)
    return acc_a * ea + acc_b * eb, m, l_a * ea + l_b * eb


def _attn_kernel(q1_ref, q4_ref, q16_ref,
                 k1_ref, k1p_ref, k4_ref, k4p_ref, k16_ref, k16p_ref,
                 v1_ref, v1p_ref, v4_ref, v4p_ref, v16_ref, v16p_ref,
                 o_ref,
                 a16_ref, m16_ref, l16_ref, a4_ref, m4_ref, l4_ref):
    n = pl.program_id(2)
    d4, d16 = DILATIONS[1], DILATIONS[2]
    sub4 = SUPER // d4
    ratio = d16 // d4
    shape = (BLOCK, HEAD_DIM)

    qi = lax.broadcasted_iota(jnp.int32, (BLOCK, 2 * BLOCK), 0)
    kj = lax.broadcasted_iota(jnp.int32, (BLOCK, 2 * BLOCK), 1)
    band = (kj >= qi) & (kj <= qi + BLOCK)
    bias_band = jnp.where(band, 0.0, NEG).astype(F32)
    bias_head = jnp.where(n == 0, jnp.where(band & (kj >= BLOCK), 0.0, NEG).astype(F32), bias_band)

    def lanes(r):
        return slice(r * HEAD_DIM, (r + 1) * HEAD_DIM)

    for r in range(d16):
        a, bb = r // d4, r % d4
        k = jnp.concatenate([k16p_ref[:, lanes(r)], k16_ref[:, lanes(r)]], axis=0)
        v = jnp.concatenate([v16p_ref[:, lanes(r)], v16_ref[:, lanes(r)]], axis=0)
        acc, m, l = _attn_block(q16_ref[:, lanes(r)], k, v, bias_head)
        rows = pl.ds(bb * sub4 + a, BLOCK, stride=ratio)
        a16_ref[rows, :] = acc
        m16_ref[rows, :] = jnp.broadcast_to(m, shape)
        l16_ref[rows, :] = jnp.broadcast_to(l, shape)

    def branch4(r, jj, k, v, bias):
        q = q4_ref[pl.ds(jj * BLOCK, BLOCK), lanes(r)]
        acc, m, l = _attn_block(q, k, v, bias)
        src = pl.ds(r * sub4 + jj * BLOCK, BLOCK)
        acc, m, l = _merge(acc, m, l, a16_ref[src, :], m16_ref[src, :], l16_ref[src, :])
        rows = pl.ds(jj * (BLOCK * d4) + r, BLOCK, stride=d4)
        a4_ref[rows, :] = acc
        m4_ref[rows, :] = m
        l4_ref[rows, :] = l

    for r in range(d4):
        k = jnp.concatenate([k4p_ref[:, lanes(r)], k4_ref[pl.ds(0, BLOCK), lanes(r)]], axis=0)
        v = jnp.concatenate([v4p_ref[:, lanes(r)], v4_ref[pl.ds(0, BLOCK), lanes(r)]], axis=0)
        branch4(r, 0, k, v, bias_head)
        for jj in range(1, sub4 // BLOCK):
            keys = pl.ds((jj - 1) * BLOCK, 2 * BLOCK)
            branch4(r, jj, k4_ref[keys, lanes(r)], v4_ref[keys, lanes(r)], bias_band)

    def branch1(ii, k, v, bias):
        acc, m, l = _attn_block(q1_ref[pl.ds(ii * BLOCK, BLOCK), :], k, v, bias)
        src = pl.ds(ii * BLOCK, BLOCK)
        acc, m, l = _merge(acc, m, l, a4_ref[src, :], m4_ref[src, :], l4_ref[src, :])
        o_ref[src, :] = (acc / l).astype(o_ref.dtype)

    k = jnp.concatenate([k1p_ref[...], k1_ref[pl.ds(0, BLOCK), :]], axis=0)
    v = jnp.concatenate([v1p_ref[...], v1_ref[pl.ds(0, BLOCK), :]], axis=0)
    branch1(0, k, v, bias_head)
    for ii in range(1, SUPER // BLOCK):
        keys = pl.ds((ii - 1) * BLOCK, 2 * BLOCK)
        branch1(ii, k1_ref[keys, :], v1_ref[keys, :], bias_band)


def _attention(z1, z4, z16, n_attn):
    b, _, s, hd = z1.shape
    d4, d16 = DILATIONS[1], DILATIONS[2]
    views = {1: z1, d4: z4, d16: z16}
    n_super = s // SUPER

    def cur(d, slot0):
        return pl.BlockSpec((None, None, SUPER // d, d * hd),
                            lambda bi, h, n: (bi, slot0 + h, n, 0))

    def prev(d, slot0):
        per = SUPER // d // BLOCK
        return pl.BlockSpec((None, None, BLOCK, d * hd),
                            lambda bi, h, n: (bi, slot0 + h, jnp.maximum(n * per - 1, 0), 0))

    in_specs = [cur(1, 0), cur(d4, 0), cur(d16, 0)]
    args = [views[1], views[d4], views[d16]]
    for slot0 in (n_attn, 2 * n_attn):
        for d in DILATIONS:
            in_specs += [cur(d, slot0), prev(d, slot0)]
            args += [views[d], views[d]]
    scratch = [pltpu.VMEM((SUPER, hd), F32) for _ in range(6)]
    return pl.pallas_call(
        _attn_kernel,
        grid=(b, n_attn, n_super),
        in_specs=in_specs,
        out_specs=pl.BlockSpec((None, SUPER, hd), lambda bi, h, n: (bi, n, h)),
        out_shape=jax.ShapeDtypeStruct((b, s, n_attn * hd), BF16),
        scratch_shapes=scratch,
        compiler_params=_cparams(("arbitrary", "arbitrary", "arbitrary")),
        name="dilated_attention",
    )(*args)


def _sgu_kernel(u_ref, v_ref, w_ref, b_ref, o_ref, *, n_chunks):
    row = lax.broadcasted_iota(jnp.int32, (CHUNK, CHUNK), 0)
    col = lax.broadcasted_iota(jnp.int32, (CHUNK, CHUNK), 1)
    w = (w_ref[...] * (row >= col).astype(F32)).astype(BF16)
    bias = b_ref[...]
    for c in range(n_chunks):
        rows = pl.ds(c * CHUNK, CHUNK)
        v = jax.nn.gelu(v_ref[rows, :].astype(F32))
        u = jax.nn.gelu(u_ref[rows, :].astype(F32))
        mixed = jnp.dot(w, v.astype(BF16), preferred_element_type=F32) + bias
        o_ref[rows, :] = (u * mixed).astype(o_ref.dtype)


def _sgu(zh, w_s, b_s, n_attn, n_sgu, rows):
    b, _, s, hd = zh.shape
    u0, v0 = 3 * n_attn, 3 * n_attn + n_sgu
    kern = functools.partial(_sgu_kernel, n_chunks=rows // CHUNK)
    return pl.pallas_call(
        kern,
        grid=(b, n_sgu, s // rows),
        in_specs=[
            pl.BlockSpec((None, None, rows, hd), lambda bi, g, n: (bi, u0 + g, n, 0)),
            pl.BlockSpec((None, None, rows, hd), lambda bi, g, n: (bi, v0 + g, n, 0)),
            pl.BlockSpec((None, CHUNK, CHUNK), lambda bi, g, n: (g, 0, 0)),
            pl.BlockSpec((None, CHUNK, 1), lambda bi, g, n: (g, 0, 0)),
        ],
        out_specs=pl.BlockSpec((None, rows, hd), lambda bi, g, n: (bi, n, g)),
        out_shape=jax.ShapeDtypeStruct((b, s, n_sgu * hd), BF16),
        compiler_params=_cparams(("arbitrary", "arbitrary", "arbitrary")),
        name="spatial_gating",
    )(zh, zh, w_s, b_s.reshape(n_sgu, CHUNK, 1))


def _outproj_kernel(x_ref, a_ref, s_ref, wa_ref, ws_ref, gate_ref, o_ref):
    acc = jnp.dot(a_ref[...], wa_ref[...], preferred_element_type=F32)
    acc += jnp.dot(s_ref[...], ws_ref[...], preferred_element_type=F32)
    o_ref[...] = x_ref[...] + gate_ref[...] * acc


def _outproj(x, attn, sgu, w_out, li, gate, tm):
    b, s, d = x.shape
    wa, ws = attn.shape[-1], sgu.shape[-1]
    tiles_per_seq = s // tm
    row = lambda i: (i // tiles_per_seq, i % tiles_per_seq, 0)
    return pl.pallas_call(
        _outproj_kernel,
        grid=(b * tiles_per_seq,),
        in_specs=[
            pl.BlockSpec((None, tm, d), row),
            pl.BlockSpec((None, tm, wa), row),
            pl.BlockSpec((None, tm, ws), row),
            pl.BlockSpec((None, wa, d), lambda i: (li, 0, 0)),
            pl.BlockSpec((None, ws, d), lambda i: (li, wa // ws, 0)),
            pl.BlockSpec((None, 1, d), lambda i: (i // tiles_per_seq, 0, 0)),
        ],
        out_specs=pl.BlockSpec((None, tm, d), row),
        out_shape=jax.ShapeDtypeStruct((b, s, d), F32),
        compiler_params=_cparams(("arbitrary",)),
        name="even_outproj",
    )(x, attn, sgu, w_out, w_out, gate)


def _conv_kernel(x_ref, xh_ref, g_ref, sc_ref, sh_ref, gate_ref, wb_ref, wc_ref, wx_ref, cw_ref,
                 wo_ref, o_ref, h_ref, *, tm, tiles_per_seq):
    i, j = pl.program_id(0), pl.program_id(1)

    @pl.when(j == 0)
    def _():
        _normmod_rows(h_ref, x_ref, g_ref, sc_ref, sh_ref, tm, offset=HALO)
        halo = _normmod(xh_ref[...], g_ref[...], sc_ref[...], sh_ref[...])
        h_ref[pl.ds(0, HALO), :] = jnp.where(i % tiles_per_seq == 0, 0.0, halo).astype(BF16)
        o_ref[...] = jnp.zeros_like(o_ref)

    h_ext = h_ref[...]
    y = (jnp.dot(h_ext, wc_ref[...], preferred_element_type=F32)
         * jnp.dot(h_ext, wx_ref[...], preferred_element_type=F32))
    cw = cw_ref[...]
    conv = cw[2:3] * y + cw[1:2] * pltpu.roll(y, 1, 0) + cw[0:1] * pltpu.roll(y, 2, 0)
    gb = jnp.dot(h_ref[pl.ds(HALO, tm), :], wb_ref[...], preferred_element_type=F32)
    a = (gb * conv[HALO:]).astype(BF16)
    o_ref[...] += jnp.dot(a, wo_ref[...], preferred_element_type=F32)

    @pl.when(j == pl.num_programs(1) - 1)
    def _():
        o_ref[...] = x_ref[...] + gate_ref[...] * o_ref[...]


def _conv_mixer(x, g, sc, sh, gate, w_in, conv_w, w_out, li, tm, tc):
    b, s, d = x.shape
    tiles_per_seq = s // tm
    nt = d // tc
    row = lambda i, j: (i // tiles_per_seq, i % tiles_per_seq, 0)
    per_b = lambda i, j: (i // tiles_per_seq, 0, 0)
    halo_blocks = tm // HALO
    halo = lambda i, j: (i // tiles_per_seq,
                         jnp.maximum((i % tiles_per_seq) * halo_blocks - 1, 0), 0)
    kern = functools.partial(_conv_kernel, tm=tm, tiles_per_seq=tiles_per_seq)
    return pl.pallas_call(
        kern,
        grid=(b * tiles_per_seq, nt),
        in_specs=[
            pl.BlockSpec((None, tm, d), row),
            pl.BlockSpec((None, HALO, d), halo),
            pl.BlockSpec((1, d), lambda i, j: (0, 0)),
            pl.BlockSpec((None, 1, d), per_b),
            pl.BlockSpec((None, 1, d), per_b),
            pl.BlockSpec((None, 1, d), per_b),
            pl.BlockSpec((None, d, tc), lambda i, j: (li, 0, j)),
            pl.BlockSpec((None, d, tc), lambda i, j: (li, 0, nt + j)),
            pl.BlockSpec((None, d, tc), lambda i, j: (li, 0, 2 * nt + j)),
            pl.BlockSpec((CONV_WIDTH, tc), lambda i, j: (0, j)),
            pl.BlockSpec((None, tc, d), lambda i, j: (li, j, 0)),
        ],
        out_specs=pl.BlockSpec((None, tm, d), row),
        out_shape=jax.ShapeDtypeStruct((b, s, d), F32),
        scratch_shapes=[pltpu.VMEM((HALO + tm, d), BF16)],
        compiler_params=_cparams(("arbitrary", "arbitrary")),
        name="conv_mixer",
    )(x, x, g, sc, sh, gate, w_in, w_in, w_in, conv_w, w_out)


def _ffn_kernel(x_ref, g_ref, sc_ref, sh_ref, gate_ref, wg_ref, wu_ref, wd_ref, fn_ref, o_ref, h_ref,
                *, final):
    f = pl.program_id(1)

    @pl.when(f == 0)
    def _():
        _normmod_rows(h_ref, x_ref, g_ref, sc_ref, sh_ref, x_ref.shape[0])
        o_ref[...] = jnp.zeros_like(o_ref)

    h = h_ref[...]
    gt = jnp.dot(h, wg_ref[...], preferred_element_type=F32)
    up = jnp.dot(h, wu_ref[...], preferred_element_type=F32)
    a = (jax.nn.silu(gt) * up).astype(BF16)
    o_ref[...] += jnp.dot(a, wd_ref[...], preferred_element_type=F32)

    @pl.when(f == pl.num_programs(1) - 1)
    def _():
        y = x_ref[...] + gate_ref[...] * o_ref[...]
        if final:
            y = (y * lax.rsqrt(jnp.mean(y * y, axis=-1, keepdims=True) + EPS)) * fn_ref[...]
        o_ref[...] = y


def _ffn(x, g, sc, sh, gate, w_gate, w_up, w_down, li, final_gain, final, tm, tf):
    b, s, d = x.shape
    dff = w_gate.shape[-1]
    tiles_per_seq = s // tm
    row = lambda i, f: (i // tiles_per_seq, i % tiles_per_seq, 0)
    per_b = lambda i, f: (i // tiles_per_seq, 0, 0)
    const = lambda i, f: (0, 0)
    kern = functools.partial(_ffn_kernel, final=final)
    return pl.pallas_call(
        kern,
        grid=(b * tiles_per_seq, dff // tf),
        in_specs=[
            pl.BlockSpec((None, tm, d), row),
            pl.BlockSpec((1, d), const),
            pl.BlockSpec((None, 1, d), per_b),
            pl.BlockSpec((None, 1, d), per_b),
            pl.BlockSpec((None, 1, d), per_b),
            pl.BlockSpec((None, d, tf), lambda i, f: (li, 0, f)),
            pl.BlockSpec((None, d, tf), lambda i, f: (li, 0, f)),
            pl.BlockSpec((None, tf, d), lambda i, f: (li, f, 0)),
            pl.BlockSpec((1, d), const),
        ],
        out_specs=pl.BlockSpec((None, tm, d), row),
        out_shape=jax.ShapeDtypeStruct((b, s, d), F32),
        scratch_shapes=[pltpu.VMEM((tm, d), BF16)],
        compiler_params=_cparams(("arbitrary", "arbitrary")),
        name="swiglu_ffn_final" if final else "swiglu_ffn",
    )(x, g, sc, sh, gate, w_gate, w_up, w_down, final_gain)


def _rope_tables(positions):
    inv_freq = ROPE_THETA ** (-jnp.arange(0, ROPE_DIM, 2, dtype=F32) / ROPE_DIM)
    ang = positions.astype(F32)[..., None] * inv_freq
    cos, sin = jnp.cos(ang), jnp.sin(ang)
    rest = HEAD_DIM - ROPE_DIM
    ones = jnp.ones(ang.shape[:-1] + (rest,), F32)
    zeros = jnp.zeros(ang.shape[:-1] + (rest,), F32)
    z_half = jnp.zeros_like(sin)
    cosf = jnp.concatenate([cos, cos, ones], axis=-1)
    sina = jnp.concatenate([-sin, z_half, zeros], axis=-1)
    sinb = jnp.concatenate([z_half, sin, zeros], axis=-1)
    return cosf, sina, sinb


def _pick(n, candidates):
    for c in candidates:
        if n % c == 0:
            return c
    return n


def kernel(x, c, positions, ada_w, ada_b, norm_mix, norm_ffn, ab_w_in, sgu_w, sgu_b, ab_w_out,
           conv_w_in, conv_w, conv_w_out, ffn_w_gate, ffn_w_up, ffn_w_down, final_norm):
    b, s, d = x.shape
    depth = ada_w.shape[0]
    n_slots = d // HEAD_DIM
    n_attn = 3 * n_slots // 4
    n_sgu = n_slots - n_attn
    dff = ffn_w_gate.shape[-1]
    assert s % SUPER == 0 and d % HEAD_DIM == 0

    tm_big = _pick(s, (1024, 512, 256, 128))
    tm_small = _pick(s, (512, 256, 128))
    tf = _pick(dff, (512, 256, 128))
    tc = _pick(d, (512, 256, 128))

    mod = _modulation(c, ada_w, ada_b)
    mod = mod.reshape(depth, b, 6, 1, d)
    rope = _rope_tables(positions)
    final_gain = final_norm.reshape(1, d)
    ab_w_in, ab_w_out, conv_w_in, conv_w_out, ffn_w_gate, ffn_w_up, ffn_w_down = (
        w.astype(BF16) for w in (ab_w_in, ab_w_out, conv_w_in, conv_w_out, ffn_w_gate, ffn_w_up,
                                 ffn_w_down))

    for layer in range(depth):
        sh_m, sc_m, g_m, sh_f, sc_f, g_f = (mod[layer, :, k] for k in range(6))
        gain_m = norm_mix[layer].reshape(1, d)
        gain_f = norm_ffn[layer].reshape(1, d)
        i = layer // 2
        if layer % 2 == 0:
            zh, z4, z16 = _inproj(x, gain_m, sc_m, sh_m, ab_w_in, i, rope, n_attn, n_sgu, tm_big)
            attn = _attention(zh, z4, z16, n_attn)
            sgu = _sgu(zh, sgu_w[i], sgu_b[i], n_attn, n_sgu, _pick(s, (2048, 1024, 512, 256, 128)))
            x = _outproj(x, attn, sgu, ab_w_out, i, g_m, tm_small)
        else:
            x = _conv_mixer(x, gain_m, sc_m, sh_m, g_m, conv_w_in, conv_w[i], conv_w_out, i,
                            tm_small, tc)
        x = _ffn(x, gain_f, sc_f, sh_f, g_f, ffn_w_gate, ffn_w_up, ffn_w_down, layer, final_gain,
                 layer == depth - 1, tm_small, tf)
    return x
```

```python
import functools
import math

import jax
import jax.numpy as jnp
from jax import lax
from jax.experimental import pallas as pl
from jax.experimental.pallas import tpu as pltpu

F32 = jnp.float32
BF16 = jnp.bfloat16

HEAD_DIM = 128
ROPE_DIM = HEAD_DIM // 4
ROPE_HALF = ROPE_DIM // 2
ROPE_THETA = 500000.0
EPS = 1e-6
CHUNK = 128
BLOCK = 128
DILATIONS = (1, 4, 16)
SUPER = BLOCK * DILATIONS[-1]
CONV_WIDTH = 3
HALO = 16
NEG = -1e30
VMEM_LIMIT_BYTES = 56 * 1024 * 1024


def _cparams(sem):
    return pltpu.CompilerParams(dimension_semantics=sem, vmem_limit_bytes=VMEM_LIMIT_BYTES)


def _normmod(x, g, sc, sh):
    y = x * lax.rsqrt(jnp.mean(x * x, axis=-1, keepdims=True) + EPS)
    return (y * g) * (1.0 + sc) + sh


NORM_ROWS = 32


def _normmod_rows(h_ref, x_ref, g_ref, sc_ref, sh_ref, rows, offset=0):
    g, sc, sh = g_ref[...], sc_ref[...], sh_ref[...]

    def body(c, carry):
        r0 = pl.multiple_of(c * NORM_ROWS, NORM_ROWS)
        h = _normmod(x_ref[pl.ds(r0, NORM_ROWS), :], g, sc, sh)
        h_ref[pl.ds(offset + r0, NORM_ROWS), :] = h.astype(h_ref.dtype)
        return carry

    lax.fori_loop(0, rows // NORM_ROWS, body, 0, unroll=2)


def _mod_kernel(c_ref, w_ref, b_ref, o_ref):
    ca = jax.nn.silu(c_ref[...]).astype(BF16)
    o_ref[...] = jnp.dot(ca, w_ref[...].astype(BF16), preferred_element_type=F32) + b_ref[...]


def _modulation(c, ada_w, ada_b):
    depth, d, n = ada_w.shape
    b = c.shape[0]
    rows = -(-b // 8) * 8
    c_pad = jnp.pad(c, ((0, rows - b), (0, 0)))
    tn = min(n, 1024)
    out = pl.pallas_call(
        _mod_kernel,
        grid=(depth, n // tn),
        in_specs=[
            pl.BlockSpec((rows, d), lambda l, j: (0, 0)),
            pl.BlockSpec((None, d, tn), lambda l, j: (l, 0, j)),
            pl.BlockSpec((None, 1, tn), lambda l, j: (l, 0, j)),
        ],
        out_specs=pl.BlockSpec((None, rows, tn), lambda l, j: (l, 0, j)),
        out_shape=jax.ShapeDtypeStruct((depth, rows, n), F32),
        compiler_params=_cparams(("arbitrary", "arbitrary")),
        name="adaln_mod",
    )(c_pad, ada_w, ada_b.reshape(depth, 1, n))
    return out[:, :b]


def _inproj_kernel(x_ref, g_ref, sc_ref, sh_ref, w_ref, cos_ref, sin_ref,
                   o1_ref, o4_ref, o16_ref, h_ref, s1_ref, s4_ref,
                   *, n_q_tiles, n_rope_tiles, n_qkv_tiles, heads_per_tile, scale, tm):
    j = pl.program_id(1)
    d4, d16 = DILATIONS[1], DILATIONS[2]
    ratio = d16 // d4

    @pl.when(j == 0)
    def _():
        _normmod_rows(h_ref, x_ref, g_ref, sc_ref, sh_ref, tm)

    acc = jnp.dot(h_ref[...], w_ref[...], preferred_element_type=F32)

    def lanes(r):
        return slice(r * HEAD_DIM, (r + 1) * HEAD_DIM)

    @pl.when(j < n_rope_tiles)
    def _():
        cs, sn = cos_ref[...], sin_ref[...]
        mult = jnp.where(j < n_q_tiles, scale, 1.0).astype(F32)
        for hh in range(heads_per_tile):
            t = acc[:, lanes(hh)]
            r = t * cs + pltpu.roll(t, HEAD_DIM // 2, 1) * sn
            s1_ref[hh] = r * mult

    @pl.when((j >= n_rope_tiles) & (j < n_qkv_tiles))
    def _():
        for hh in range(heads_per_tile):
            s1_ref[hh] = acc[:, lanes(hh)]

    @pl.when(j < n_qkv_tiles)
    def _():
        for hh in range(heads_per_tile):
            o1_ref[hh] = s1_ref[hh].astype(BF16)
            for r in range(d4):
                part = s1_ref[hh, pl.ds(r, tm // d4, stride=d4), :]
                o4_ref[hh, :, lanes(r)] = part.astype(BF16)
                s4_ref[hh, pl.ds(r * (tm // d4), tm // d4), :] = part
            for r in range(d16):
                a, bb = r // d4, r % d4
                part = s4_ref[hh, pl.ds(bb * (tm // d4) + a, tm // d16, stride=ratio), :]
                o16_ref[hh, :, lanes(r)] = part.astype(BF16)

    @pl.when(j >= n_qkv_tiles)
    def _():
        for hh in range(heads_per_tile):
            o1_ref[hh] = acc[:, lanes(hh)].astype(BF16)


def _inproj(x, g, sc, sh, w, li, rope, n_attn, n_sgu, tm):
    b, s, d = x.shape
    n = w.shape[-1]
    n_slots = n // HEAD_DIM
    hpt = math.gcd(math.gcd(n_attn, n_sgu), 4)
    tn = hpt * HEAD_DIM
    tiles_per_seq = s // tm
    d4, d16 = DILATIONS[1], DILATIONS[2]
    n_qkv_tiles = 3 * n_attn // hpt
    cosf, sinf = rope
    kern = functools.partial(
        _inproj_kernel, n_q_tiles=n_attn // hpt, n_rope_tiles=2 * n_attn // hpt,
        n_qkv_tiles=n_qkv_tiles, heads_per_tile=hpt, scale=HEAD_DIM ** -0.5, tm=tm)
    row = lambda i, j: (i // tiles_per_seq, i % tiles_per_seq, 0)
    per_b = lambda i, j: (i // tiles_per_seq, 0, 0)
    qkv_out = lambda i, j: (i // tiles_per_seq, jnp.minimum(j, n_qkv_tiles - 1), i % tiles_per_seq, 0)
    return pl.pallas_call(
        kern,
        grid=(b * tiles_per_seq, n_slots // hpt),
        in_specs=[
            pl.BlockSpec((None, tm, d), row),
            pl.BlockSpec((1, d), lambda i, j: (0, 0)),
            pl.BlockSpec((None, 1, d), per_b),
            pl.BlockSpec((None, 1, d), per_b),
            pl.BlockSpec((None, d, tn), lambda i, j: (li, 0, j)),
            pl.BlockSpec((None, tm, HEAD_DIM), row),
            pl.BlockSpec((None, tm, HEAD_DIM), row),
        ],
        out_specs=[
            pl.BlockSpec((None, hpt, tm, HEAD_DIM),
                         lambda i, j: (i // tiles_per_seq, j, i % tiles_per_seq, 0)),
            pl.BlockSpec((None, hpt, tm // d4, d4 * HEAD_DIM), qkv_out),
            pl.BlockSpec((None, hpt, tm // d16, d16 * HEAD_DIM), qkv_out),
        ],
        out_shape=[
            jax.ShapeDtypeStruct((b, n_slots, s, HEAD_DIM), BF16),
            jax.ShapeDtypeStruct((b, 3 * n_attn, s // d4, d4 * HEAD_DIM), BF16),
            jax.ShapeDtypeStruct((b, 3 * n_attn, s // d16, d16 * HEAD_DIM), BF16),
        ],
        scratch_shapes=[
            pltpu.VMEM((tm, d), BF16),
            pltpu.VMEM((hpt, tm, HEAD_DIM), F32),
            pltpu.VMEM((hpt, tm, HEAD_DIM), F32),
        ],
        compiler_params=_cparams(("arbitrary", "arbitrary")),
        name="even_inproj",
    )(x, g, sc, sh, w, cosf, sinf)


def _attn_block(q, k, v, bias):
    s = lax.dot_general(q, k, (((1,), (1,)), ((), ())), preferred_element_type=F32) + bias
    m = jnp.max(s, axis=1, keepdims=True)
    p = jnp.exp(s - m)
    l = jnp.sum(p, axis=1, keepdims=True)
    acc = jnp.dot(p.astype(BF16), v, preferred_element_type=F32)
    return acc, m, l


def _merge(acc_a, m_a, l_a, acc_b, m_b, l_b):
    m = jnp.maximum(m_a, m_b)
    ea = jnp.exp(m_a - m)
    eb = jnp.exp(m_b - m)
    return acc_a * ea + acc_b * eb, m, l_a * ea + l_b * eb


def _attn_kernel(q1_ref, q4_ref, q16_ref,
                 k1_ref, k1p_ref, k4_ref, k4p_ref, k16_ref, k16p_ref,
                 v1_ref, v1p_ref, v4_ref, v4p_ref, v16_ref, v16p_ref,
                 o_ref,
                 a16_ref, m16_ref, l16_ref, a4_ref, m4_ref, l4_ref):
    n = pl.program_id(2)
    d4, d16 = DILATIONS[1], DILATIONS[2]
    sub4 = SUPER // d4
    ratio = d16 // d4
    shape = (BLOCK, HEAD_DIM)

    qi = lax.broadcasted_iota(jnp.int32, (BLOCK, 2 * BLOCK), 0)
    kj = lax.broadcasted_iota(jnp.int32, (BLOCK, 2 * BLOCK), 1)
    band = (kj >= qi) & (kj <= qi + BLOCK)
    bias_band = jnp.where(band, 0.0, NEG).astype(F32)
    bias_head = jnp.where(n == 0, jnp.where(band & (kj >= BLOCK), 0.0, NEG).astype(F32), bias_band)

    def lanes(r):
        return slice(r * HEAD_DIM, (r + 1) * HEAD_DIM)

    for r in range(d16):
        a, bb = r // d4, r % d4
        k = jnp.concatenate([k16p_ref[:, lanes(r)], k16_ref[:, lanes(r)]], axis=0)
        v = jnp.concatenate([v16p_ref[:, lanes(r)], v16_ref[:, lanes(r)]], axis=0)
        acc, m, l = _attn_block(q16_ref[:, lanes(r)], k, v, bias_head)
        rows = pl.ds(bb * sub4 + a, BLOCK, stride=ratio)
        a16_ref[rows, :] = acc
        m16_ref[rows, :] = jnp.broadcast_to(m, shape)
        l16_ref[rows, :] = jnp.broadcast_to(l, shape)

    def branch4(r, jj, k, v, bias):
        q = q4_ref[pl.ds(jj * BLOCK, BLOCK), lanes(r)]
        acc, m, l = _attn_block(q, k, v, bias)
        src = pl.ds(r * sub4 + jj * BLOCK, BLOCK)
        acc, m, l = _merge(acc, m, l, a16_ref[src, :], m16_ref[src, :], l16_ref[src, :])
        rows = pl.ds(jj * (BLOCK * d4) + r, BLOCK, stride=d4)
        a4_ref[rows, :] = acc
        m4_ref[rows, :] = m
        l4_ref[rows, :] = l

    for r in range(d4):
        k = jnp.concatenate([k4p_ref[:, lanes(r)], k4_ref[pl.ds(0, BLOCK), lanes(r)]], axis=0)
        v = jnp.concatenate([v4p_ref[:, lanes(r)], v4_ref[pl.ds(0, BLOCK), lanes(r)]], axis=0)
        branch4(r, 0, k, v, bias_head)
        for jj in range(1, sub4 // BLOCK):
            keys = pl.ds((jj - 1) * BLOCK, 2 * BLOCK)
            branch4(r, jj, k4_ref[keys, lanes(r)], v4_ref[keys, lanes(r)], bias_band)

    def branch1(ii, k, v, bias):
        acc, m, l = _attn_block(q1_ref[pl.ds(ii * BLOCK, BLOCK), :], k, v, bias)
        src = pl.ds(ii * BLOCK, BLOCK)
        acc, m, l = _merge(acc, m, l, a4_ref[src, :], m4_ref[src, :], l4_ref[src, :])
        o_ref[src, :] = (acc / l).astype(o_ref.dtype)

    k = jnp.concatenate([k1p_ref[...], k1_ref[pl.ds(0, BLOCK), :]], axis=0)
    v = jnp.concatenate([v1p_ref[...], v1_ref[pl.ds(0, BLOCK), :]], axis=0)
    branch1(0, k, v, bias_head)
    for ii in range(1, SUPER // BLOCK):
        keys = pl.ds((ii - 1) * BLOCK, 2 * BLOCK)
        branch1(ii, k1_ref[keys, :], v1_ref[keys, :], bias_band)


def _attention(z1, z4, z16, n_attn):
    b, _, s, hd = z1.shape
    d4, d16 = DILATIONS[1], DILATIONS[2]
    views = {1: z1, d4: z4, d16: z16}
    n_super = s // SUPER

    def cur(d, slot0):
        return pl.BlockSpec((None, None, SUPER // d, d * hd),
                            lambda bi, h, n: (bi, slot0 + h, n, 0))

    def prev(d, slot0):
        per = SUPER // d // BLOCK
        return pl.BlockSpec((None, None, BLOCK, d * hd),
                            lambda bi, h, n: (bi, slot0 + h, jnp.maximum(n * per - 1, 0), 0))

    in_specs = [cur(1, 0), cur(d4, 0), cur(d16, 0)]
    args = [views[1], views[d4], views[d16]]
    for slot0 in (n_attn, 2 * n_attn):
        for d in DILATIONS:
            in_specs += [cur(d, slot0), prev(d, slot0)]
            args += [views[d], views[d]]
    scratch = [pltpu.VMEM((SUPER, hd), F32) for _ in range(6)]
    return pl.pallas_call(
        _attn_kernel,
        grid=(b, n_attn, n_super),
        in_specs=in_specs,
        out_specs=pl.BlockSpec((None, SUPER, hd), lambda bi, h, n: (bi, n, h)),
        out_shape=jax.ShapeDtypeStruct((b, s, n_attn * hd), BF16),
        scratch_shapes=scratch,
        compiler_params=_cparams(("arbitrary", "arbitrary", "arbitrary")),
        name="dilated_attention",
    )(*args)


def _sgu_kernel(u_ref, v_ref, w_ref, b_ref, o_ref, *, n_chunks):
    row = lax.broadcasted_iota(jnp.int32, (CHUNK, CHUNK), 0)
    col = lax.broadcasted_iota(jnp.int32, (CHUNK, CHUNK), 1)
    w = (w_ref[...] * (row >= col).astype(F32)).astype(BF16)
    bias = b_ref[...]
    for c in range(n_chunks):
        rows = pl.ds(c * CHUNK, CHUNK)
        v = jax.nn.gelu(v_ref[rows, :].astype(F32))
        u = jax.nn.gelu(u_ref[rows, :].astype(F32))
        mixed = jnp.dot(w, v.astype(BF16), preferred_element_type=F32) + bias
        o_ref[rows, :] = (u * mixed).astype(o_ref.dtype)


def _sgu(zh, w_s, b_s, n_attn, n_sgu, rows):
    b, _, s, hd = zh.shape
    u0, v0 = 3 * n_attn, 3 * n_attn + n_sgu
    kern = functools.partial(_sgu_kernel, n_chunks=rows // CHUNK)
    return pl.pallas_call(
        kern,
        grid=(b, n_sgu, s // rows),
        in_specs=[
            pl.BlockSpec((None, None, rows, hd), lambda bi, g, n: (bi, u0 + g, n, 0)),
            pl.BlockSpec((None, None, rows, hd), lambda bi, g, n: (bi, v0 + g, n, 0)),
            pl.BlockSpec((None, CHUNK, CHUNK), lambda bi, g, n: (g, 0, 0)),
            pl.BlockSpec((None, CHUNK, 1), lambda bi, g, n: (g, 0, 0)),
        ],
        out_specs=pl.BlockSpec((None, rows, hd), lambda bi, g, n: (bi, n, g)),
        out_shape=jax.ShapeDtypeStruct((b, s, n_sgu * hd), BF16),
        compiler_params=_cparams(("arbitrary", "arbitrary", "arbitrary")),
        name="spatial_gating",
    )(zh, zh, w_s, b_s.reshape(n_sgu, CHUNK, 1))


def _outproj_kernel(x_ref, a_ref, s_ref, wa_ref, ws_ref, gate_ref, o_ref):
    acc = jnp.dot(a_ref[...], wa_ref[...], preferred_element_type=F32)
    acc += jnp.dot(s_ref[...], ws_ref[...], preferred_element_type=F32)
    o_ref[...] = x_ref[...] + gate_ref[...] * acc


def _outproj(x, attn, sgu, w_out, li, gate, tm):
    b, s, d = x.shape
    wa, ws = attn.shape[-1], sgu.shape[-1]
    tiles_per_seq = s // tm
    row = lambda i: (i // tiles_per_seq, i % tiles_per_seq, 0)
    return pl.pallas_call(
        _outproj_kernel,
        grid=(b * tiles_per_seq,),
        in_specs=[
            pl.BlockSpec((None, tm, d), row),
            pl.BlockSpec((None, tm, wa), row),
            pl.BlockSpec((None, tm, ws), row),
            pl.BlockSpec((None, wa, d), lambda i: (li, 0, 0)),
            pl.BlockSpec((None, ws, d), lambda i: (li, wa // ws, 0)),
            pl.BlockSpec((None, 1, d), lambda i: (i // tiles_per_seq, 0, 0)),
        ],
        out_specs=pl.BlockSpec((None, tm, d), row),
        out_shape=jax.ShapeDtypeStruct((b, s, d), F32),
        compiler_params=_cparams(("arbitrary",)),
        name="even_outproj",
    )(x, attn, sgu, w_out, w_out, gate)


def _conv_kernel(x_ref, xh_ref, g_ref, sc_ref, sh_ref, gate_ref, wb_ref, wc_ref, wx_ref, cw_ref,
                 wo_ref, o_ref, h_ref, *, tm, tiles_per_seq):
    i, j = pl.program_id(0), pl.program_id(1)

    @pl.when(j == 0)
    def _():
        _normmod_rows(h_ref, x_ref, g_ref, sc_ref, sh_ref, tm, offset=HALO)
        halo = _normmod(xh_ref[...], g_ref[...], sc_ref[...], sh_ref[...])
        h_ref[pl.ds(0, HALO), :] = jnp.where(i % tiles_per_seq == 0, 0.0, halo).astype(BF16)
        o_ref[...] = jnp.zeros_like(o_ref)

    h_ext = h_ref[...]
    y = (jnp.dot(h_ext, wc_ref[...], preferred_element_type=F32)
         * jnp.dot(h_ext, wx_ref[...], preferred_element_type=F32))
    cw = cw_ref[...]
    conv = cw[2:3] * y + cw[1:2] * pltpu.roll(y, 1, 0) + cw[0:1] * pltpu.roll(y, 2, 0)
    gb = jnp.dot(h_ref[pl.ds(HALO, tm), :], wb_ref[...], preferred_element_type=F32)
    a = (gb * conv[HALO:]).astype(BF16)
    o_ref[...] += jnp.dot(a, wo_ref[...], preferred_element_type=F32)

    @pl.when(j == pl.num_programs(1) - 1)
    def _():
        o_ref[...] = x_ref[...] + gate_ref[...] * o_ref[...]


def _conv_mixer(x, g, sc, sh, gate, w_in, conv_w, w_out, li, tm, tc):
    b, s, d = x.shape
    tiles_per_seq = s // tm
    nt = d // tc
    row = lambda i, j: (i // tiles_per_seq, i % tiles_per_seq, 0)
    per_b = lambda i, j: (i // tiles_per_seq, 0, 0)
    halo_blocks = tm // HALO
    halo = lambda i, j: (i // tiles_per_seq,
                         jnp.maximum((i % tiles_per_seq) * halo_blocks - 1, 0), 0)
    kern = functools.partial(_conv_kernel, tm=tm, tiles_per_seq=tiles_per_seq)
    return pl.pallas_call(
        kern,
        grid=(b * tiles_per_seq, nt),
        in_specs=[
            pl.BlockSpec((None, tm, d), row),
            pl.BlockSpec((None, HALO, d), halo),
            pl.BlockSpec((1, d), lambda i, j: (0, 0)),
            pl.BlockSpec((None, 1, d), per_b),
            pl.BlockSpec((None, 1, d), per_b),
            pl.BlockSpec((None, 1, d), per_b),
            pl.BlockSpec((None, d, tc), lambda i, j: (li, 0, j)),
            pl.BlockSpec((None, d, tc), lambda i, j: (li, 0, nt + j)),
            pl.BlockSpec((None, d, tc), lambda i, j: (li, 0, 2 * nt + j)),
            pl.BlockSpec((CONV_WIDTH, tc), lambda i, j: (0, j)),
            pl.BlockSpec((None, tc, d), lambda i, j: (li, j, 0)),
        ],
        out_specs=pl.BlockSpec((None, tm, d), row),
        out_shape=jax.ShapeDtypeStruct((b, s, d), F32),
        scratch_shapes=[pltpu.VMEM((HALO + tm, d), BF16)],
        compiler_params=_cparams(("arbitrary", "arbitrary")),
        name="conv_mixer",
    )(x, x, g, sc, sh, gate, w_in, w_in, w_in, conv_w, w_out)


FFN_NORM_FROM = 2


def _ffn_kernel(x_hbm, g_ref, sc_ref, sh_ref, scn_ref, shn_ref, gate_ref, wg_ref, wu_ref, wd_ref,
                fn_ref, o_ref, xbuf, h_ref, hn_ref, sem, *, final, n_f, tm, tiles_per_seq, n_tiles,
                chunks_per_step):
    i, f = pl.program_id(0), pl.program_id(1)
    has_next = i + 1 < n_tiles

    def x_copy(tile):
        rows = pl.ds(pl.multiple_of((tile % tiles_per_seq) * tm, tm), tm)
        return pltpu.make_async_copy(x_hbm.at[tile // tiles_per_seq, rows, :], xbuf, sem)

    def step():
        h = h_ref[...]
        gt = jnp.dot(h, wg_ref[...], preferred_element_type=F32)
        up = jnp.dot(h, wu_ref[...], preferred_element_type=F32)
        a = (jax.nn.silu(gt) * up).astype(BF16)
        o_ref[...] += gate_ref[...] * jnp.dot(a, wd_ref[...], preferred_element_type=F32)

    def norm_next_rows():
        g, sc, sh = g_ref[...], scn_ref[...], shn_ref[...]
        last = tm // NORM_ROWS - 1
        for k in range(chunks_per_step):
            c = jnp.minimum((f - FFN_NORM_FROM) * chunks_per_step + k, last)
            r0 = pl.multiple_of(c * NORM_ROWS, NORM_ROWS)
            hn = _normmod(xbuf[pl.ds(r0, NORM_ROWS), :], g, sc, sh)
            hn_ref[pl.ds(r0, NORM_ROWS), :] = hn.astype(BF16)

    @pl.when(f == 0)
    def _():
        @pl.when(i == 0)
        def _():
            first = x_copy(0)
            first.start()
            first.wait()
            _normmod_rows(h_ref, xbuf, g_ref, sc_ref, sh_ref, tm)

        @pl.when(i > 0)
        def _():
            h_ref[...] = hn_ref[...]

        o_ref[...] = xbuf[...]

        @pl.when(has_next)
        def _():
            x_copy(i + 1).start()

    @pl.when(f < FFN_NORM_FROM)
    def _():
        step()

        @pl.when((f == FFN_NORM_FROM - 1) & has_next)
        def _():
            x_copy(i + 1).wait()

    @pl.when(f >= FFN_NORM_FROM)
    def _():
        step()
        norm_next_rows()

    if final:
        @pl.when(f == n_f - 1)
        def _():
            y = o_ref[...]
            o_ref[...] = (y * lax.rsqrt(jnp.mean(y * y, axis=-1, keepdims=True) + EPS)) * fn_ref[...]


def _ffn(x, g, sc, sh, gate, w_gate, w_up, w_down, li, final_gain, final, tm, tf):
    b, s, d = x.shape
    dff = w_gate.shape[-1]
    n_f = dff // tf
    tiles_per_seq = s // tm
    n_tiles = b * tiles_per_seq
    n_chunks = tm // NORM_ROWS
    norm_steps = n_f - FFN_NORM_FROM
    assert norm_steps >= 1
    chunks_per_step = -(-n_chunks // norm_steps)
    row = lambda i, f: (i // tiles_per_seq, i % tiles_per_seq, 0)
    per_b = lambda i, f: (i // tiles_per_seq, 0, 0)
    nxt = lambda i: jnp.minimum(i + 1, n_tiles - 1)
    per_b_next = lambda i, f: (nxt(i) // tiles_per_seq, 0, 0)
    const = lambda i, f: (0, 0)
    kern = functools.partial(_ffn_kernel, final=final, n_f=n_f, tm=tm, tiles_per_seq=tiles_per_seq,
                             n_tiles=n_tiles, chunks_per_step=chunks_per_step)
    return pl.pallas_call(
        kern,
        grid=(n_tiles, n_f),
        in_specs=[
            pl.BlockSpec(memory_space=pl.ANY),
            pl.BlockSpec((1, d), const),
            pl.BlockSpec((None, 1, d), per_b),
            pl.BlockSpec((None, 1, d), per_b),
            pl.BlockSpec((None, 1, d), per_b_next),
            pl.BlockSpec((None, 1, d), per_b_next),
            pl.BlockSpec((None, 1, d), per_b),
            pl.BlockSpec((None, d, tf), lambda i, f: (li, 0, f)),
            pl.BlockSpec((None, d, tf), lambda i, f: (li, 0, f)),
            pl.BlockSpec((None, tf, d), lambda i, f: (li, f, 0)),
            pl.BlockSpec((1, d), const),
        ],
        out_specs=pl.BlockSpec((None, tm, d), row),
        out_shape=jax.ShapeDtypeStruct((b, s, d), F32),
        scratch_shapes=[
            pltpu.VMEM((tm, d), F32),
            pltpu.VMEM((tm, d), BF16),
            pltpu.VMEM((tm, d), BF16),
            pltpu.SemaphoreType.DMA(()),
        ],
        compiler_params=_cparams(("arbitrary", "arbitrary")),
        name="swiglu_ffn_final" if final else "swiglu_ffn",
    )(x, g, sc, sh, sc, sh, gate, w_gate, w_up, w_down, final_gain)


def _rope_tables(positions):
    inv_freq = ROPE_THETA ** (-jnp.arange(0, ROPE_DIM, 2, dtype=F32) / ROPE_DIM)
    ang = positions.astype(F32)[..., None] * inv_freq
    cos, sin = jnp.cos(ang), jnp.sin(ang)
    gap_one = jnp.ones(ang.shape[:-1] + (HEAD_DIM // 2 - ROPE_HALF,), F32)
    gap_zero = jnp.zeros_like(gap_one)
    cosf = jnp.concatenate([cos, gap_one, cos, gap_one], axis=-1)
    sinf = jnp.concatenate([-sin, gap_zero, sin, gap_zero], axis=-1)
    return cosf, sinf


def _rope_head_order():
    fill = HEAD_DIM // 2 - ROPE_HALF
    return (list(range(0, ROPE_HALF)) + list(range(ROPE_DIM, ROPE_DIM + fill))
            + list(range(ROPE_HALF, ROPE_DIM)) + list(range(ROPE_DIM + fill, HEAD_DIM)))


def _permute_qk_columns(w_in, n_attn):
    lead = w_in.shape[:-1]
    n_qk = 2 * n_attn * HEAD_DIM
    order = _rope_head_order()
    qk = w_in[..., :n_qk].reshape(lead + (2 * n_attn, HEAD_DIM))
    runs, start = [], 0
    for pos in range(1, HEAD_DIM + 1):
        if pos == HEAD_DIM or order[pos] != order[pos - 1] + 1:
            runs.append(qk[..., order[start]:order[pos - 1] + 1])
            start = pos
    qk = jnp.concatenate(runs, axis=-1).reshape(lead + (n_qk,))
    return jnp.concatenate([qk, w_in[..., n_qk:]], axis=-1)


def _pick(n, candidates):
    for c in candidates:
        if n % c == 0:
            return c
    return n


def kernel(x, c, positions, ada_w, ada_b, norm_mix, norm_ffn, ab_w_in, sgu_w, sgu_b, ab_w_out,
           conv_w_in, conv_w, conv_w_out, ffn_w_gate, ffn_w_up, ffn_w_down, final_norm):
    b, s, d = x.shape
    depth = ada_w.shape[0]
    n_slots = d // HEAD_DIM
    n_attn = 3 * n_slots // 4
    n_sgu = n_slots - n_attn
    dff = ffn_w_gate.shape[-1]
    assert s % SUPER == 0 and d % HEAD_DIM == 0

    tm_big = _pick(s, (1024, 512, 256, 128))
    tm_small = _pick(s, (512, 256, 128))
    tf = _pick(dff, (512, 256, 128))
    tc = _pick(d, (512, 256, 128))

    mod = _modulation(c, ada_w, ada_b)
    mod = mod.reshape(depth, b, 6, 1, d)
    rope = _rope_tables(positions)
    final_gain = final_norm.reshape(1, d)
    ab_w_in, ab_w_out, conv_w_in, conv_w_out, ffn_w_gate, ffn_w_up, ffn_w_down = (
        w.astype(BF16) for w in (_permute_qk_columns(ab_w_in, n_attn), ab_w_out, conv_w_in,
                                 conv_w_out, ffn_w_gate, ffn_w_up, ffn_w_down))

    for layer in range(depth):
        sh_m, sc_m, g_m, sh_f, sc_f, g_f = (mod[layer, :, k] for k in range(6))
        gain_m = norm_mix[layer].reshape(1, d)
        gain_f = norm_ffn[layer].reshape(1, d)
        i = layer // 2
        if layer % 2 == 0:
            zh, z4, z16 = _inproj(x, gain_m, sc_m, sh_m, ab_w_in, i, rope, n_attn, n_sgu, tm_big)
            attn = _attention(zh, z4, z16, n_attn)
            sgu = _sgu(zh, sgu_w[i], sgu_b[i], n_attn, n_sgu, _pick(s, (2048, 1024, 512, 256, 128)))
            x = _outproj(x, attn, sgu, ab_w_out, i, g_m, tm_small)
        else:
            x = _conv_mixer(x, gain_m, sc_m, sh_m, g_m, conv_w_in, conv_w[i], conv_w_out, i,
                            tm_small, tc)
        x = _ffn(x, gain_f, sc_f, sh_f, g_f, ffn_w_gate, ffn_w_up, ffn_w_down, layer, final_gain,
                 layer == depth - 1, tm_big, tf)
    return x
```

```python
import functools
import math

import jax
import jax.numpy as jnp
from jax import lax
from jax.experimental import pallas as pl
from jax.experimental.pallas import tpu as pltpu

F32 = jnp.float32
BF16 = jnp.bfloat16

HEAD_DIM = 128
ROPE_DIM = HEAD_DIM // 4
ROPE_HALF = ROPE_DIM // 2
ROPE_THETA = 500000.0
EPS = 1e-6
CHUNK = 128
BLOCK = 128
DILATIONS = (1, 4, 16)
SUPER = BLOCK * DILATIONS[-1]
CONV_WIDTH = 3
HALO = 16
NEG = -1e30
VMEM_LIMIT_BYTES = 56 * 1024 * 1024


def _cparams(sem):
    return pltpu.CompilerParams(dimension_semantics=sem, vmem_limit_bytes=VMEM_LIMIT_BYTES)


def _normmod(x, g, sc, sh):
    y = x * lax.rsqrt(jnp.mean(x * x, axis=-1, keepdims=True) + EPS)
    return (y * g) * (1.0 + sc) + sh


NORM_ROWS = 32


def _normmod_rows(h_ref, x_ref, g_ref, sc_ref, sh_ref, rows, offset=0):
    g, sc, sh = g_ref[...], sc_ref[...], sh_ref[...]

    def body(c, carry):
        r0 = pl.multiple_of(c * NORM_ROWS, NORM_ROWS)
        h = _normmod(x_ref[pl.ds(r0, NORM_ROWS), :], g, sc, sh)
        h_ref[pl.ds(offset + r0, NORM_ROWS), :] = h.astype(h_ref.dtype)
        return carry

    lax.fori_loop(0, rows // NORM_ROWS, body, 0, unroll=2)


NORM_FROM = 1


def _norm_next_rows(hn_ref, xbuf, g_ref, sc_ref, sh_ref, step, chunks_per_step, tm):
    g, sc, sh = g_ref[...], sc_ref[...], sh_ref[...]
    last = tm // NORM_ROWS - 1
    bits = None
    for k in range(chunks_per_step):
        c = jnp.minimum(step * chunks_per_step + k, last)
        r0 = pl.multiple_of(c * NORM_ROWS, NORM_ROWS)
        hn = _normmod(xbuf[pl.ds(r0, NORM_ROWS), :], g, sc, sh)
        hn_ref[pl.ds(r0, NORM_ROWS), :] = hn.astype(hn_ref.dtype)
        word = lax.bitcast_convert_type(hn[0:8, 0:HEAD_DIM], jnp.uint32)
        bits = word if bits is None else bits | word
    zero = lax.shift_right_logical(lax.shift_right_logical(bits, jnp.uint32(16)), jnp.uint32(16))
    return 1.0 + jnp.max(zero.astype(F32), axis=(0, 1), keepdims=True)


def _mod_kernel(c_ref, w_ref, b_ref, o_ref):
    ca = jax.nn.silu(c_ref[...]).astype(BF16)
    o_ref[...] = jnp.dot(ca, w_ref[...].astype(BF16), preferred_element_type=F32) + b_ref[...]


def _modulation(c, ada_w, ada_b):
    depth, d, n = ada_w.shape
    b = c.shape[0]
    rows = -(-b // 8) * 8
    c_pad = jnp.pad(c, ((0, rows - b), (0, 0)))
    tn = min(n, 1024)
    out = pl.pallas_call(
        _mod_kernel,
        grid=(depth, n // tn),
        in_specs=[
            pl.BlockSpec((rows, d), lambda l, j: (0, 0)),
            pl.BlockSpec((None, d, tn), lambda l, j: (l, 0, j)),
            pl.BlockSpec((None, 1, tn), lambda l, j: (l, 0, j)),
        ],
        out_specs=pl.BlockSpec((None, rows, tn), lambda l, j: (l, 0, j)),
        out_shape=jax.ShapeDtypeStruct((depth, rows, n), F32),
        compiler_params=_cparams(("arbitrary", "arbitrary")),
        name="adaln_mod",
    )(c_pad, ada_w, ada_b.reshape(depth, 1, n))
    return out[:, :b]


def _inproj_kernel(x_hbm, g_ref, sc_ref, sh_ref, scn_ref, shn_ref, w_ref, cos_ref, sin_ref,
                   o1_ref, o4_ref, o16_ref, xbuf, h_ref, hn_ref, acc_ref, s1_ref, s4_ref, sem,
                   *, n_q_tiles, n_rope_tiles, n_qkv_tiles, heads_per_tile, scale, tm, tiles_per_seq,
                   n_tiles, chunks_per_step):
    i, j = pl.program_id(0), pl.program_id(1)
    d4, d16 = DILATIONS[1], DILATIONS[2]
    ratio = d16 // d4
    has_next = i + 1 < n_tiles

    def x_copy(tile):
        rows = pl.ds(pl.multiple_of((tile % tiles_per_seq) * tm, tm), tm)
        return pltpu.make_async_copy(x_hbm.at[tile // tiles_per_seq, rows, :], xbuf, sem)

    @pl.when(j == 0)
    def _():
        @pl.when(i == 0)
        def _():
            first = x_copy(0)
            first.start()
            first.wait()
            _normmod_rows(h_ref, xbuf, g_ref, sc_ref, sh_ref, tm)

        @pl.when(i > 0)
        def _():
            h_ref[...] = hn_ref[...]

        @pl.when(has_next)
        def _():
            x_copy(i + 1).start()

    @pl.when(j < NORM_FROM)
    def _():
        acc_ref[...] = jnp.dot(h_ref[...], w_ref[...], preferred_element_type=F32)

        @pl.when((j == NORM_FROM - 1) & has_next)
        def _():
            x_copy(i + 1).wait()

    @pl.when(j >= NORM_FROM)
    def _():
        one = _norm_next_rows(hn_ref, xbuf, g_ref, scn_ref, shn_ref, j - NORM_FROM, chunks_per_step, tm)
        acc = jnp.dot(h_ref[...], w_ref[...], preferred_element_type=F32)
        split = acc.shape[1] - HEAD_DIM
        acc_ref[:, :split] = acc[:, :split]
        acc_ref[:, split:] = acc[:, split:] * one

    def lanes(r):
        return slice(r * HEAD_DIM, (r + 1) * HEAD_DIM)

    @pl.when(j < n_rope_tiles)
    def _():
        cs, sn = cos_ref[...], sin_ref[...]
        mult = jnp.where(j < n_q_tiles, scale, 1.0).astype(F32)
        for hh in range(heads_per_tile):
            t = acc_ref[:, lanes(hh)]
            r = t * cs + pltpu.roll(t, HEAD_DIM // 2, 1) * sn
            s1_ref[hh] = r * mult

    @pl.when((j >= n_rope_tiles) & (j < n_qkv_tiles))
    def _():
        for hh in range(heads_per_tile):
            s1_ref[hh] = acc_ref[:, lanes(hh)]

    @pl.when(j < n_qkv_tiles)
    def _():
        for hh in range(heads_per_tile):
            o1_ref[hh] = s1_ref[hh].astype(BF16)
            for r in range(d4):
                part = s1_ref[hh, pl.ds(r, tm // d4, stride=d4), :]
                o4_ref[hh, :, lanes(r)] = part.astype(BF16)
                s4_ref[hh, pl.ds(r * (tm // d4), tm // d4), :] = part
            for r in range(d16):
                a, bb = r // d4, r % d4
                part = s4_ref[hh, pl.ds(bb * (tm // d4) + a, tm // d16, stride=ratio), :]
                o16_ref[hh, :, lanes(r)] = part.astype(BF16)

    @pl.when(j >= n_qkv_tiles)
    def _():
        for hh in range(heads_per_tile):
            o1_ref[hh] = acc_ref[:, lanes(hh)].astype(BF16)


def _inproj(x, g, sc, sh, w, li, rope, n_attn, n_sgu, tm):
    b, s, d = x.shape
    n = w.shape[-1]
    n_slots = n // HEAD_DIM
    hpt = math.gcd(math.gcd(n_attn, n_sgu), 4)
    tn = hpt * HEAD_DIM
    tiles_per_seq = s // tm
    d4, d16 = DILATIONS[1], DILATIONS[2]
    n_qkv_tiles = 3 * n_attn // hpt
    n_tiles = b * tiles_per_seq
    norm_steps = n_slots // hpt - NORM_FROM
    assert norm_steps >= 1
    chunks_per_step = -(-(tm // NORM_ROWS) // norm_steps)
    cosf, sinf = rope
    kern = functools.partial(
        _inproj_kernel, n_q_tiles=n_attn // hpt, n_rope_tiles=2 * n_attn // hpt,
        n_qkv_tiles=n_qkv_tiles, heads_per_tile=hpt, scale=HEAD_DIM ** -0.5, tm=tm,
        tiles_per_seq=tiles_per_seq, n_tiles=n_tiles, chunks_per_step=chunks_per_step)
    row = lambda i, j: (i // tiles_per_seq, i % tiles_per_seq, 0)
    per_b = lambda i, j: (i // tiles_per_seq, 0, 0)
    nxt = lambda i: jnp.minimum(i + 1, n_tiles - 1)
    per_b_next = lambda i, j: (nxt(i) // tiles_per_seq, 0, 0)
    qkv_out = lambda i, j: (i // tiles_per_seq, jnp.minimum(j, n_qkv_tiles - 1), i % tiles_per_seq, 0)
    return pl.pallas_call(
        kern,
        grid=(n_tiles, n_slots // hpt),
        in_specs=[
            pl.BlockSpec(memory_space=pl.ANY),
            pl.BlockSpec((1, d), lambda i, j: (0, 0)),
            pl.BlockSpec((None, 1, d), per_b),
            pl.BlockSpec((None, 1, d), per_b),
            pl.BlockSpec((None, 1, d), per_b_next),
            pl.BlockSpec((None, 1, d), per_b_next),
            pl.BlockSpec((None, d, tn), lambda i, j: (li, 0, j)),
            pl.BlockSpec((None, tm, HEAD_DIM), row),
            pl.BlockSpec((None, tm, HEAD_DIM), row),
        ],
        out_specs=[
            pl.BlockSpec((None, hpt, tm, HEAD_DIM),
                         lambda i, j: (i // tiles_per_seq, j, i % tiles_per_seq, 0)),
            pl.BlockSpec((None, hpt, tm // d4, d4 * HEAD_DIM), qkv_out),
            pl.BlockSpec((None, hpt, tm // d16, d16 * HEAD_DIM), qkv_out),
        ],
        out_shape=[
            jax.ShapeDtypeStruct((b, n_slots, s, HEAD_DIM), BF16),
            jax.ShapeDtypeStruct((b, 3 * n_attn, s // d4, d4 * HEAD_DIM), BF16),
            jax.ShapeDtypeStruct((b, 3 * n_attn, s // d16, d16 * HEAD_DIM), BF16),
        ],
        scratch_shapes=[
            pltpu.VMEM((tm, d), F32),
            pltpu.VMEM((tm, d), BF16),
            pltpu.VMEM((tm, d), BF16),
            pltpu.VMEM((tm, tn), F32),
            pltpu.VMEM((hpt, tm, HEAD_DIM), F32),
            pltpu.VMEM((hpt, tm, HEAD_DIM), F32),
            pltpu.SemaphoreType.DMA(()),
        ],
        compiler_params=_cparams(("arbitrary", "arbitrary")),
        name="even_inproj",
    )(x, g, sc, sh, sc, sh, w, cosf, sinf)


def _attn_block(q, k, v, bias):
    s = lax.dot_general(q, k, (((1,), (1,)), ((), ())), preferred_element_type=F32) + bias
    m = jnp.max(s, axis=1, keepdims=True)
    p = jnp.exp(s - m)
    l = jnp.sum(p, axis=1, keepdims=True)
    acc = jnp.dot(p.astype(BF16), v, preferred_element_type=F32)
    return acc, m, l


def _merge(acc_a, m_a, l_a, acc_b, m_b, l_b):
    m = jnp.maximum(m_a, m_b)
    ea = jnp.exp(m_a - m)
    eb = jnp.exp(m_b - m)
    return acc_a * ea + acc_b * eb, m, l_a * ea + l_b * eb


def _attn_kernel(q1_ref, q4_ref, q16_ref,
                 k1_ref, k1p_ref, k4_ref, k4p_ref, k16_ref, k16p_ref,
                 v1_ref, v1p_ref, v4_ref, v4p_ref, v16_ref, v16p_ref,
                 o_ref,
                 a16_ref, m16_ref, l16_ref, a4_ref, m4_ref, l4_ref):
    n = pl.program_id(2)
    d4, d16 = DILATIONS[1], DILATIONS[2]
    sub4 = SUPER // d4
    ratio = d16 // d4
    shape = (BLOCK, HEAD_DIM)

    qi = lax.broadcasted_iota(jnp.int32, (BLOCK, 2 * BLOCK), 0)
    kj = lax.broadcasted_iota(jnp.int32, (BLOCK, 2 * BLOCK), 1)
    band = (kj >= qi) & (kj <= qi + BLOCK)
    bias_band = jnp.where(band, 0.0, NEG).astype(F32)
    bias_head = jnp.where(n == 0, jnp.where(band & (kj >= BLOCK), 0.0, NEG).astype(F32), bias_band)

    def lanes(r):
        return slice(r * HEAD_DIM, (r + 1) * HEAD_DIM)

    for r in range(d16):
        a, bb = r // d4, r % d4
        k = jnp.concatenate([k16p_ref[:, lanes(r)], k16_ref[:, lanes(r)]], axis=0)
        v = jnp.concatenate([v16p_ref[:, lanes(r)], v16_ref[:, lanes(r)]], axis=0)
        acc, m, l = _attn_block(q16_ref[:, lanes(r)], k, v, bias_head)
        rows = pl.ds(bb * sub4 + a, BLOCK, stride=ratio)
        a16_ref[rows, :] = acc
        m16_ref[rows, :] = jnp.broadcast_to(m, shape)
        l16_ref[rows, :] = jnp.broadcast_to(l, shape)

    def branch4(r, jj, k, v, bias):
        q = q4_ref[pl.ds(jj * BLOCK, BLOCK), lanes(r)]
        acc, m, l = _attn_block(q, k, v, bias)
        src = pl.ds(r * sub4 + jj * BLOCK, BLOCK)
        acc, m, l = _merge(acc, m, l, a16_ref[src, :], m16_ref[src, :], l16_ref[src, :])
        rows = pl.ds(jj * (BLOCK * d4) + r, BLOCK, stride=d4)
        a4_ref[rows, :] = acc
        m4_ref[rows, :] = m
        l4_ref[rows, :] = l

    for r in range(d4):
        k = jnp.concatenate([k4p_ref[:, lanes(r)], k4_ref[pl.ds(0, BLOCK), lanes(r)]], axis=0)
        v = jnp.concatenate([v4p_ref[:, lanes(r)], v4_ref[pl.ds(0, BLOCK), lanes(r)]], axis=0)
        branch4(r, 0, k, v, bias_head)
        for jj in range(1, sub4 // BLOCK):
            keys = pl.ds((jj - 1) * BLOCK, 2 * BLOCK)
            branch4(r, jj, k4_ref[keys, lanes(r)], v4_ref[keys, lanes(r)], bias_band)

    def branch1(ii, k, v, bias):
        acc, m, l = _attn_block(q1_ref[pl.ds(ii * BLOCK, BLOCK), :], k, v, bias)
        src = pl.ds(ii * BLOCK, BLOCK)
        acc, m, l = _merge(acc, m, l, a4_ref[src, :], m4_ref[src, :], l4_ref[src, :])
        o_ref[src, :] = (acc / l).astype(o_ref.dtype)

    k = jnp.concatenate([k1p_ref[...], k1_ref[pl.ds(0, BLOCK), :]], axis=0)
    v = jnp.concatenate([v1p_ref[...], v1_ref[pl.ds(0, BLOCK), :]], axis=0)
    branch1(0, k, v, bias_head)
    for ii in range(1, SUPER // BLOCK):
        keys = pl.ds((ii - 1) * BLOCK, 2 * BLOCK)
        branch1(ii, k1_ref[keys, :], v1_ref[keys, :], bias_band)


def _attention(z1, z4, z16, n_attn):
    b, _, s, hd = z1.shape
    d4, d16 = DILATIONS[1], DILATIONS[2]
    views = {1: z1, d4: z4, d16: z16}
    n_super = s // SUPER

    def cur(d, slot0):
        return pl.BlockSpec((None, None, SUPER // d, d * hd),
                            lambda bi, h, n: (bi, slot0 + h, n, 0))

    def prev(d, slot0):
        per = SUPER // d // BLOCK
        return pl.BlockSpec((None, None, BLOCK, d * hd),
                            lambda bi, h, n: (bi, slot0 + h, jnp.maximum(n * per - 1, 0), 0))

    in_specs = [cur(1, 0), cur(d4, 0), cur(d16, 0)]
    args = [views[1], views[d4], views[d16]]
    for slot0 in (n_attn, 2 * n_attn):
        for d in DILATIONS:
            in_specs += [cur(d, slot0), prev(d, slot0)]
            args += [views[d], views[d]]
    scratch = [pltpu.VMEM((SUPER, hd), F32) for _ in range(6)]
    return pl.pallas_call(
        _attn_kernel,
        grid=(b, n_attn, n_super),
        in_specs=in_specs,
        out_specs=pl.BlockSpec((None, SUPER, hd), lambda bi, h, n: (bi, n, h)),
        out_shape=jax.ShapeDtypeStruct((b, s, n_attn * hd), BF16),
        scratch_shapes=scratch,
        compiler_params=_cparams(("arbitrary", "arbitrary", "arbitrary")),
        name="dilated_attention",
    )(*args)


def _sgu_kernel(u_ref, v_ref, w_ref, b_ref, o_ref, *, n_chunks):
    row = lax.broadcasted_iota(jnp.int32, (CHUNK, CHUNK), 0)
    col = lax.broadcasted_iota(jnp.int32, (CHUNK, CHUNK), 1)
    w = (w_ref[...] * (row >= col).astype(F32)).astype(BF16)
    bias = b_ref[...]
    for c in range(n_chunks):
        rows = pl.ds(c * CHUNK, CHUNK)
        v = jax.nn.gelu(v_ref[rows, :].astype(F32))
        u = jax.nn.gelu(u_ref[rows, :].astype(F32))
        mixed = jnp.dot(w, v.astype(BF16), preferred_element_type=F32) + bias
        o_ref[rows, :] = (u * mixed).astype(o_ref.dtype)


def _sgu(zh, w_s, b_s, n_attn, n_sgu, rows):
    b, _, s, hd = zh.shape
    u0, v0 = 3 * n_attn, 3 * n_attn + n_sgu
    kern = functools.partial(_sgu_kernel, n_chunks=rows // CHUNK)
    return pl.pallas_call(
        kern,
        grid=(b, n_sgu, s // rows),
        in_specs=[
            pl.BlockSpec((None, None, rows, hd), lambda bi, g, n: (bi, u0 + g, n, 0)),
            pl.BlockSpec((None, None, rows, hd), lambda bi, g, n: (bi, v0 + g, n, 0)),
            pl.BlockSpec((None, CHUNK, CHUNK), lambda bi, g, n: (g, 0, 0)),
            pl.BlockSpec((None, CHUNK, 1), lambda bi, g, n: (g, 0, 0)),
        ],
        out_specs=pl.BlockSpec((None, rows, hd), lambda bi, g, n: (bi, n, g)),
        out_shape=jax.ShapeDtypeStruct((b, s, n_sgu * hd), BF16),
        compiler_params=_cparams(("arbitrary", "arbitrary", "arbitrary")),
        name="spatial_gating",
    )(zh, zh, w_s, b_s.reshape(n_sgu, CHUNK, 1))


def _outproj_kernel(x_ref, a_ref, s_ref, wa_ref, ws_ref, gate_ref, o_ref):
    acc = jnp.dot(a_ref[...], wa_ref[...], preferred_element_type=F32)
    acc += jnp.dot(s_ref[...], ws_ref[...], preferred_element_type=F32)
    o_ref[...] = x_ref[...] + gate_ref[...] * acc


def _outproj(x, attn, sgu, w_out, li, gate, tm):
    b, s, d = x.shape
    wa, ws = attn.shape[-1], sgu.shape[-1]
    tiles_per_seq = s // tm
    row = lambda i: (i // tiles_per_seq, i % tiles_per_seq, 0)
    return pl.pallas_call(
        _outproj_kernel,
        grid=(b * tiles_per_seq,),
        in_specs=[
            pl.BlockSpec((None, tm, d), row),
            pl.BlockSpec((None, tm, wa), row),
            pl.BlockSpec((None, tm, ws), row),
            pl.BlockSpec((None, wa, d), lambda i: (li, 0, 0)),
            pl.BlockSpec((None, ws, d), lambda i: (li, wa // ws, 0)),
            pl.BlockSpec((None, 1, d), lambda i: (i // tiles_per_seq, 0, 0)),
        ],
        out_specs=pl.BlockSpec((None, tm, d), row),
        out_shape=jax.ShapeDtypeStruct((b, s, d), F32),
        compiler_params=_cparams(("arbitrary",)),
        name="even_outproj",
    )(x, attn, sgu, w_out, w_out, gate)


def _conv_kernel(x_hbm, g_ref, sc_ref, sh_ref, scn_ref, shn_ref, gate_ref, wb_ref, wc_ref, wx_ref,
                 cw_ref, wo_ref, o_ref, xbuf, h_ref, hn_ref, sem, *, tm, tiles_per_seq, n_tiles,
                 chunks_per_step):
    i, j = pl.program_id(0), pl.program_id(1)
    has_next = i + 1 < n_tiles

    def x_copy(tile):
        rows = pl.ds(pl.multiple_of((tile % tiles_per_seq) * tm, tm), tm)
        return pltpu.make_async_copy(x_hbm.at[tile // tiles_per_seq, rows, :], xbuf, sem)

    def step(one=None):
        h_ext = h_ref[...]
        y = (jnp.dot(h_ext, wc_ref[...], preferred_element_type=F32)
             * jnp.dot(h_ext, wx_ref[...], preferred_element_type=F32))
        cw = cw_ref[...]
        conv = cw[2:3] * y + cw[1:2] * pltpu.roll(y, 1, 0) + cw[0:1] * pltpu.roll(y, 2, 0)
        gb = jnp.dot(h_ref[pl.ds(HALO, tm), :], wb_ref[...], preferred_element_type=F32)
        if one is not None:
            gb = gb * one
        a = (gb * conv[HALO:]).astype(BF16)
        o_ref[...] += gate_ref[...] * jnp.dot(a, wo_ref[...], preferred_element_type=F32)

    @pl.when(j == 0)
    def _():
        @pl.when(i == 0)
        def _():
            first = x_copy(0)
            first.start()
            first.wait()
            _normmod_rows(h_ref, xbuf, g_ref, sc_ref, sh_ref, tm, offset=HALO)

        @pl.when(i > 0)
        def _():
            h_ref[pl.ds(0, HALO), :] = h_ref[pl.ds(tm, HALO), :]
            h_ref[pl.ds(HALO, tm), :] = hn_ref[...]

        @pl.when(i % tiles_per_seq == 0)
        def _():
            h_ref[pl.ds(0, HALO), :] = jnp.zeros((HALO, h_ref.shape[1]), BF16)

        o_ref[...] = xbuf[...]

        @pl.when(has_next)
        def _():
            x_copy(i + 1).start()

    @pl.when(j < NORM_FROM)
    def _():
        step()

        @pl.when((j == NORM_FROM - 1) & has_next)
        def _():
            x_copy(i + 1).wait()

    @pl.when(j >= NORM_FROM)
    def _():
        step(_norm_next_rows(hn_ref, xbuf, g_ref, scn_ref, shn_ref, j - NORM_FROM, chunks_per_step, tm))


def _conv_mixer(x, g, sc, sh, gate, w_in, conv_w, w_out, li, tm, tc):
    b, s, d = x.shape
    tiles_per_seq = s // tm
    n_tiles = b * tiles_per_seq
    nt = d // tc
    norm_steps = nt - NORM_FROM
    assert norm_steps >= 1
    chunks_per_step = -(-(tm // NORM_ROWS) // norm_steps)
    row = lambda i, j: (i // tiles_per_seq, i % tiles_per_seq, 0)
    per_b = lambda i, j: (i // tiles_per_seq, 0, 0)
    nxt = lambda i: jnp.minimum(i + 1, n_tiles - 1)
    per_b_next = lambda i, j: (nxt(i) // tiles_per_seq, 0, 0)
    kern = functools.partial(_conv_kernel, tm=tm, tiles_per_seq=tiles_per_seq, n_tiles=n_tiles,
                             chunks_per_step=chunks_per_step)
    return pl.pallas_call(
        kern,
        grid=(n_tiles, nt),
        in_specs=[
            pl.BlockSpec(memory_space=pl.ANY),
            pl.BlockSpec((1, d), lambda i, j: (0, 0)),
            pl.BlockSpec((None, 1, d), per_b),
            pl.BlockSpec((None, 1, d), per_b),
            pl.BlockSpec((None, 1, d), per_b_next),
            pl.BlockSpec((None, 1, d), per_b_next),
            pl.BlockSpec((None, 1, d), per_b),
            pl.BlockSpec((None, d, tc), lambda i, j: (li, 0, j)),
            pl.BlockSpec((None, d, tc), lambda i, j: (li, 0, nt + j)),
            pl.BlockSpec((None, d, tc), lambda i, j: (li, 0, 2 * nt + j)),
            pl.BlockSpec((CONV_WIDTH, tc), lambda i, j: (0, j)),
            pl.BlockSpec((None, tc, d), lambda i, j: (li, j, 0)),
        ],
        out_specs=pl.BlockSpec((None, tm, d), row),
        out_shape=jax.ShapeDtypeStruct((b, s, d), F32),
        scratch_shapes=[
            pltpu.VMEM((tm, d), F32),
            pltpu.VMEM((HALO + tm, d), BF16),
            pltpu.VMEM((tm, d), BF16),
            pltpu.SemaphoreType.DMA(()),
        ],
        compiler_params=_cparams(("arbitrary", "arbitrary")),
        name="conv_mixer",
    )(x, g, sc, sh, sc, sh, gate, w_in, w_in, w_in, conv_w, w_out)


def _ffn_kernel(x_hbm, g_ref, sc_ref, sh_ref, scn_ref, shn_ref, gate_ref, wg_ref, wu_ref, wd_ref,
                fn_ref, o_ref, xbuf, h_ref, hn_ref, sem, *, final, n_f, tm, tiles_per_seq, n_tiles,
                chunks_per_step):
    i, f = pl.program_id(0), pl.program_id(1)
    has_next = i + 1 < n_tiles

    def x_copy(tile):
        rows = pl.ds(pl.multiple_of((tile % tiles_per_seq) * tm, tm), tm)
        return pltpu.make_async_copy(x_hbm.at[tile // tiles_per_seq, rows, :], xbuf, sem)

    def step(one=None):
        h = h_ref[...]
        gt = jnp.dot(h, wg_ref[...], preferred_element_type=F32)
        up = jnp.dot(h, wu_ref[...], preferred_element_type=F32)
        if one is not None:
            up = up * one
        a = (jax.nn.silu(gt) * up).astype(BF16)
        o_ref[...] += gate_ref[...] * jnp.dot(a, wd_ref[...], preferred_element_type=F32)

    @pl.when(f == 0)
    def _():
        @pl.when(i == 0)
        def _():
            first = x_copy(0)
            first.start()
            first.wait()
            _normmod_rows(h_ref, xbuf, g_ref, sc_ref, sh_ref, tm)

        @pl.when(i > 0)
        def _():
            h_ref[...] = hn_ref[...]

        o_ref[...] = xbuf[...]

        @pl.when(has_next)
        def _():
            x_copy(i + 1).start()

    @pl.when(f < NORM_FROM)
    def _():
        step()

        @pl.when((f == NORM_FROM - 1) & has_next)
        def _():
            x_copy(i + 1).wait()

    @pl.when(f >= NORM_FROM)
    def _():
        step(_norm_next_rows(hn_ref, xbuf, g_ref, scn_ref, shn_ref, f - NORM_FROM, chunks_per_step, tm))

    if final:
        @pl.when(f == n_f - 1)
        def _():
            y = o_ref[...]
            o_ref[...] = (y * lax.rsqrt(jnp.mean(y * y, axis=-1, keepdims=True) + EPS)) * fn_ref[...]


def _ffn(x, g, sc, sh, gate, w_gate, w_up, w_down, li, final_gain, final, tm, tf):
    b, s, d = x.shape
    dff = w_gate.shape[-1]
    n_f = dff // tf
    tiles_per_seq = s // tm
    n_tiles = b * tiles_per_seq
    n_chunks = tm // NORM_ROWS
    norm_steps = n_f - NORM_FROM
    assert norm_steps >= 1
    chunks_per_step = -(-n_chunks // norm_steps)
    row = lambda i, f: (i // tiles_per_seq, i % tiles_per_seq, 0)
    per_b = lambda i, f: (i // tiles_per_seq, 0, 0)
    nxt = lambda i: jnp.minimum(i + 1, n_tiles - 1)
    per_b_next = lambda i, f: (nxt(i) // tiles_per_seq, 0, 0)
    const = lambda i, f: (0, 0)
    kern = functools.partial(_ffn_kernel, final=final, n_f=n_f, tm=tm, tiles_per_seq=tiles_per_seq,
                             n_tiles=n_tiles, chunks_per_step=chunks_per_step)
    return pl.pallas_call(
        kern,
        grid=(n_tiles, n_f),
        in_specs=[
            pl.BlockSpec(memory_space=pl.ANY),
            pl.BlockSpec((1, d), const),
            pl.BlockSpec((None, 1, d), per_b),
            pl.BlockSpec((None, 1, d), per_b),
            pl.BlockSpec((None, 1, d), per_b_next),
            pl.BlockSpec((None, 1, d), per_b_next),
            pl.BlockSpec((None, 1, d), per_b),
            pl.BlockSpec((None, d, tf), lambda i, f: (li, 0, f)),
            pl.BlockSpec((None, d, tf), lambda i, f: (li, 0, f)),
            pl.BlockSpec((None, tf, d), lambda i, f: (li, f, 0)),
            pl.BlockSpec((1, d), const),
        ],
        out_specs=pl.BlockSpec((None, tm, d), row),
        out_shape=jax.ShapeDtypeStruct((b, s, d), F32),
        scratch_shapes=[
            pltpu.VMEM((tm, d), F32),
            pltpu.VMEM((tm, d), BF16),
            pltpu.VMEM((tm, d), BF16),
            pltpu.SemaphoreType.DMA(()),
        ],
        compiler_params=_cparams(("arbitrary", "arbitrary")),
        name="swiglu_ffn_final" if final else "swiglu_ffn",
    )(x, g, sc, sh, sc, sh, gate, w_gate, w_up, w_down, final_gain)


def _rope_tables(positions):
    inv_freq = ROPE_THETA ** (-jnp.arange(0, ROPE_DIM, 2, dtype=F32) / ROPE_DIM)
    ang = positions.astype(F32)[..., None] * inv_freq
    cos, sin = jnp.cos(ang), jnp.sin(ang)
    gap_one = jnp.ones(ang.shape[:-1] + (HEAD_DIM // 2 - ROPE_HALF,), F32)
    gap_zero = jnp.zeros_like(gap_one)
    cosf = jnp.concatenate([cos, gap_one, cos, gap_one], axis=-1)
    sinf = jnp.concatenate([-sin, gap_zero, sin, gap_zero], axis=-1)
    return cosf, sinf


def _rope_head_order():
    fill = HEAD_DIM // 2 - ROPE_HALF
    return (list(range(0, ROPE_HALF)) + list(range(ROPE_DIM, ROPE_DIM + fill))
            + list(range(ROPE_HALF, ROPE_DIM)) + list(range(ROPE_DIM + fill, HEAD_DIM)))


def _permute_qk_columns(w_in, n_attn):
    lead = w_in.shape[:-1]
    n_qk = 2 * n_attn * HEAD_DIM
    order = jnp.asarray(_rope_head_order(), jnp.int32)
    select = (jnp.arange(HEAD_DIM, dtype=jnp.int32)[:, None] == order[None, :]).astype(w_in.dtype)
    qk = w_in[..., :n_qk].reshape(lead + (2 * n_attn, HEAD_DIM))
    qk = jnp.einsum("...hc,ce->...he", qk, select, preferred_element_type=F32).astype(w_in.dtype)
    return jnp.concatenate([qk.reshape(lead + (n_qk,)), w_in[..., n_qk:]], axis=-1)


def _pick(n, candidates):
    for c in candidates:
        if n % c == 0:
            return c
    return n


def kernel(x, c, positions, ada_w, ada_b, norm_mix, norm_ffn, ab_w_in, sgu_w, sgu_b, ab_w_out,
           conv_w_in, conv_w, conv_w_out, ffn_w_gate, ffn_w_up, ffn_w_down, final_norm):
    b, s, d = x.shape
    depth = ada_w.shape[0]
    n_slots = d // HEAD_DIM
    n_attn = 3 * n_slots // 4
    n_sgu = n_slots - n_attn
    dff = ffn_w_gate.shape[-1]
    assert s % SUPER == 0 and d % HEAD_DIM == 0

    tm_big = _pick(s, (1024, 512, 256, 128))
    tm_small = _pick(s, (512, 256, 128))
    tf = _pick(dff, (512, 256, 128))
    tc = _pick(d // 4, (512, 256, 128))

    mod = _modulation(c, ada_w, ada_b)
    mod = mod.reshape(depth, b, 6, 1, d)
    rope = _rope_tables(positions)
    final_gain = final_norm.reshape(1, d)
    ab_w_in, ab_w_out, conv_w_in, conv_w_out, ffn_w_gate, ffn_w_up, ffn_w_down = (
        w.astype(BF16) for w in (ab_w_in, ab_w_out, conv_w_in, conv_w_out, ffn_w_gate, ffn_w_up,
                                 ffn_w_down))
    ab_w_in = _permute_qk_columns(ab_w_in, n_attn)

    for layer in range(depth):
        sh_m, sc_m, g_m, sh_f, sc_f, g_f = (mod[layer, :, k] for k in range(6))
        gain_m = norm_mix[layer].reshape(1, d)
        gain_f = norm_ffn[layer].reshape(1, d)
        i = layer // 2
        if layer % 2 == 0:
            zh, z4, z16 = _inproj(x, gain_m, sc_m, sh_m, ab_w_in, i, rope, n_attn, n_sgu, tm_big)
            attn = _attention(zh, z4, z16, n_attn)
            sgu = _sgu(zh, sgu_w[i], sgu_b[i], n_attn, n_sgu, _pick(s, (2048, 1024, 512, 256, 128)))
            x = _outproj(x, attn, sgu, ab_w_out, i, g_m, tm_small)
        else:
            x = _conv_mixer(x, gain_m, sc_m, sh_m, g_m, conv_w_in, conv_w[i], conv_w_out, i,
                            tm_small, tc)
        x = _ffn(x, gain_f, sc_f, sh_f, g_f, ffn_w_gate, ffn_w_up, ffn_w_down, layer, final_gain,
                 layer == depth - 1, tm_big, tf)
    return x
```

```python
import functools
import math

import jax
import jax.numpy as jnp
from jax import lax
from jax.experimental import pallas as pl
from jax.experimental.pallas import tpu as pltpu

F32 = jnp.float32
BF16 = jnp.bfloat16

HEAD_DIM = 128
ROPE_DIM = HEAD_DIM // 4
ROPE_HALF = ROPE_DIM // 2
ROPE_THETA = 500000.0
EPS = 1e-6
CHUNK = 128
BLOCK = 128
DILATIONS = (1, 4, 16)
SUPER = BLOCK * DILATIONS[-1]
CONV_WIDTH = 3
HALO = 16
NEG = -1e30
VMEM_LIMIT_BYTES = 58 * 1024 * 1024


def _cparams(sem):
    return pltpu.CompilerParams(dimension_semantics=sem, vmem_limit_bytes=VMEM_LIMIT_BYTES)


def _normmod(x, g, sc, sh):
    y = x * lax.rsqrt(jnp.mean(x * x, axis=-1, keepdims=True) + EPS)
    return (y * g) * (1.0 + sc) + sh


NORM_ROWS = 32


def _normmod_rows(h_ref, x_ref, g_ref, sc_ref, sh_ref, rows, offset=0):
    g, sc, sh = g_ref[...], sc_ref[...], sh_ref[...]

    def body(c, carry):
        r0 = pl.multiple_of(c * NORM_ROWS, NORM_ROWS)
        h = _normmod(x_ref[pl.ds(r0, NORM_ROWS), :], g, sc, sh)
        h_ref[pl.ds(offset + r0, NORM_ROWS), :] = h.astype(h_ref.dtype)
        return carry

    lax.fori_loop(0, rows // NORM_ROWS, body, 0, unroll=2)


NORM_FROM = 1


def _norm_next_rows(hn_ref, xbuf, g_ref, sc_ref, sh_ref, step, chunks_per_step, tm):
    g, sc, sh = g_ref[...], sc_ref[...], sh_ref[...]
    last = tm // NORM_ROWS - 1
    bits = None
    for k in range(chunks_per_step):
        c = jnp.minimum(step * chunks_per_step + k, last)
        r0 = pl.multiple_of(c * NORM_ROWS, NORM_ROWS)
        hn = _normmod(xbuf[pl.ds(r0, NORM_ROWS), :], g, sc, sh)
        hn_ref[pl.ds(r0, NORM_ROWS), :] = hn.astype(hn_ref.dtype)
        word = lax.bitcast_convert_type(hn[0:8, 0:HEAD_DIM], jnp.uint32)
        bits = word if bits is None else bits | word
    zero = lax.shift_right_logical(lax.shift_right_logical(bits, jnp.uint32(16)), jnp.uint32(16))
    return 1.0 + jnp.max(zero.astype(F32), axis=(0, 1), keepdims=True)


def _mod_kernel(c_ref, w_ref, b_ref, o_ref):
    ca = jax.nn.silu(c_ref[...]).astype(BF16)
    o_ref[...] = jnp.dot(ca, w_ref[...].astype(BF16), preferred_element_type=F32) + b_ref[...]


def _modulation(c, ada_w, ada_b):
    depth, d, n = ada_w.shape
    b = c.shape[0]
    rows = -(-b // 8) * 8
    c_pad = jnp.pad(c, ((0, rows - b), (0, 0)))
    tn = min(n, 1024)
    out = pl.pallas_call(
        _mod_kernel,
        grid=(depth, n // tn),
        in_specs=[
            pl.BlockSpec((rows, d), lambda l, j: (0, 0)),
            pl.BlockSpec((None, d, tn), lambda l, j: (l, 0, j)),
            pl.BlockSpec((None, 1, tn), lambda l, j: (l, 0, j)),
        ],
        out_specs=pl.BlockSpec((None, rows, tn), lambda l, j: (l, 0, j)),
        out_shape=jax.ShapeDtypeStruct((depth, rows, n), F32),
        compiler_params=_cparams(("arbitrary", "arbitrary")),
        name="adaln_mod",
    )(c_pad, ada_w, ada_b.reshape(depth, 1, n))
    return out[:, :b]


def _inproj_kernel(x_hbm, g_ref, sc_ref, sh_ref, scn_ref, shn_ref, w_ref, cos_ref, sin_ref,
                   o1_ref, o4_ref, o16_ref, xbuf, h_ref, hn_ref, acc_ref, s1_ref, s4_ref, sem,
                   *, n_q_tiles, n_rope_tiles, n_qkv_tiles, heads_per_tile, scale, tm, tiles_per_seq,
                   n_tiles, chunks_per_step):
    i, j = pl.program_id(0), pl.program_id(1)
    d4, d16 = DILATIONS[1], DILATIONS[2]
    ratio = d16 // d4
    has_next = i + 1 < n_tiles

    def x_copy(tile):
        rows = pl.ds(pl.multiple_of((tile % tiles_per_seq) * tm, tm), tm)
        return pltpu.make_async_copy(x_hbm.at[tile // tiles_per_seq, rows, :], xbuf, sem)

    @pl.when(j == 0)
    def _():
        @pl.when(i == 0)
        def _():
            first = x_copy(0)
            first.start()
            first.wait()
            _normmod_rows(h_ref, xbuf, g_ref, sc_ref, sh_ref, tm)

        @pl.when(i > 0)
        def _():
            h_ref[...] = hn_ref[...]

        @pl.when(has_next)
        def _():
            x_copy(i + 1).start()

    @pl.when(j < NORM_FROM)
    def _():
        acc_ref[...] = jnp.dot(h_ref[...], w_ref[...], preferred_element_type=F32)

        @pl.when((j == NORM_FROM - 1) & has_next)
        def _():
            x_copy(i + 1).wait()

    @pl.when(j >= NORM_FROM)
    def _():
        one = _norm_next_rows(hn_ref, xbuf, g_ref, scn_ref, shn_ref, j - NORM_FROM, chunks_per_step, tm)
        acc = jnp.dot(h_ref[...], w_ref[...], preferred_element_type=F32)
        split = acc.shape[1] - HEAD_DIM
        acc_ref[:, :split] = acc[:, :split]
        acc_ref[:, split:] = acc[:, split:] * one

    def lanes(r):
        return slice(r * HEAD_DIM, (r + 1) * HEAD_DIM)

    @pl.when(j < n_rope_tiles)
    def _():
        cs, sn = cos_ref[...], sin_ref[...]
        mult = jnp.where(j < n_q_tiles, scale, 1.0).astype(F32)
        for hh in range(heads_per_tile):
            t = acc_ref[:, lanes(hh)]
            r = t * cs + pltpu.roll(t, HEAD_DIM // 2, 1) * sn
            s1_ref[hh] = r * mult

    @pl.when((j >= n_rope_tiles) & (j < n_qkv_tiles))
    def _():
        for hh in range(heads_per_tile):
            s1_ref[hh] = acc_ref[:, lanes(hh)]

    @pl.when(j < n_qkv_tiles)
    def _():
        for hh in range(heads_per_tile):
            o1_ref[hh] = s1_ref[hh].astype(BF16)
            for r in range(d4):
                part = s1_ref[hh, pl.ds(r, tm // d4, stride=d4), :]
                o4_ref[hh, :, lanes(r)] = part.astype(BF16)
                s4_ref[hh, pl.ds(r * (tm // d4), tm // d4), :] = part
            for r in range(d16):
                a, bb = r // d4, r % d4
                part = s4_ref[hh, pl.ds(bb * (tm // d4) + a, tm // d16, stride=ratio), :]
                o16_ref[hh, :, lanes(r)] = part.astype(BF16)

    @pl.when(j >= n_qkv_tiles)
    def _():
        for hh in range(heads_per_tile):
            o1_ref[hh] = acc_ref[:, lanes(hh)].astype(BF16)


def _inproj(x, g, sc, sh, w, li, rope, n_attn, n_sgu, tm):
    b, s, d = x.shape
    n = w.shape[-1]
    n_slots = n // HEAD_DIM
    hpt = math.gcd(math.gcd(n_attn, n_sgu), 4)
    tn = hpt * HEAD_DIM
    tiles_per_seq = s // tm
    d4, d16 = DILATIONS[1], DILATIONS[2]
    n_qkv_tiles = 3 * n_attn // hpt
    n_tiles = b * tiles_per_seq
    norm_steps = n_slots // hpt - NORM_FROM
    assert norm_steps >= 1
    chunks_per_step = -(-(tm // NORM_ROWS) // norm_steps)
    cosf, sinf = rope
    kern = functools.partial(
        _inproj_kernel, n_q_tiles=n_attn // hpt, n_rope_tiles=2 * n_attn // hpt,
        n_qkv_tiles=n_qkv_tiles, heads_per_tile=hpt, scale=HEAD_DIM ** -0.5, tm=tm,
        tiles_per_seq=tiles_per_seq, n_tiles=n_tiles, chunks_per_step=chunks_per_step)
    row = lambda i, j: (i // tiles_per_seq, i % tiles_per_seq, 0)
    per_b = lambda i, j: (i // tiles_per_seq, 0, 0)
    nxt = lambda i: jnp.minimum(i + 1, n_tiles - 1)
    per_b_next = lambda i, j: (nxt(i) // tiles_per_seq, 0, 0)
    qkv_out = lambda i, j: (i // tiles_per_seq, jnp.minimum(j, n_qkv_tiles - 1), i % tiles_per_seq, 0)
    return pl.pallas_call(
        kern,
        grid=(n_tiles, n_slots // hpt),
        in_specs=[
            pl.BlockSpec(memory_space=pl.ANY),
            pl.BlockSpec((1, d), lambda i, j: (0, 0)),
            pl.BlockSpec((None, 1, d), per_b),
            pl.BlockSpec((None, 1, d), per_b),
            pl.BlockSpec((None, 1, d), per_b_next),
            pl.BlockSpec((None, 1, d), per_b_next),
            pl.BlockSpec((None, d, tn), lambda i, j: (li, 0, j)),
            pl.BlockSpec((None, tm, HEAD_DIM), row),
            pl.BlockSpec((None, tm, HEAD_DIM), row),
        ],
        out_specs=[
            pl.BlockSpec((None, hpt, tm, HEAD_DIM),
                         lambda i, j: (i // tiles_per_seq, j, i % tiles_per_seq, 0)),
            pl.BlockSpec((None, hpt, tm // d4, d4 * HEAD_DIM), qkv_out),
            pl.BlockSpec((None, hpt, tm // d16, d16 * HEAD_DIM), qkv_out),
        ],
        out_shape=[
            jax.ShapeDtypeStruct((b, n_slots, s, HEAD_DIM), BF16),
            jax.ShapeDtypeStruct((b, 3 * n_attn, s // d4, d4 * HEAD_DIM), BF16),
            jax.ShapeDtypeStruct((b, 3 * n_attn, s // d16, d16 * HEAD_DIM), BF16),
        ],
        scratch_shapes=[
            pltpu.VMEM((tm, d), F32),
            pltpu.VMEM((tm, d), BF16),
            pltpu.VMEM((tm, d), BF16),
            pltpu.VMEM((tm, tn), F32),
            pltpu.VMEM((hpt, tm, HEAD_DIM), F32),
            pltpu.VMEM((hpt, tm, HEAD_DIM), F32),
            pltpu.SemaphoreType.DMA(()),
        ],
        compiler_params=_cparams(("arbitrary", "arbitrary")),
        name="even_inproj",
    )(x, g, sc, sh, sc, sh, w, cosf, sinf)


def _attn_block(q, k, v, bias):
    s = lax.dot_general(q, k, (((1,), (1,)), ((), ())), preferred_element_type=F32) + bias
    m = jnp.max(s, axis=1, keepdims=True)
    p = jnp.exp(s - m)
    l = jnp.sum(p, axis=1, keepdims=True)
    acc = jnp.dot(p.astype(BF16), v, preferred_element_type=F32)
    return acc, m, l


def _merge(acc_a, m_a, l_a, acc_b, m_b, l_b):
    m = jnp.maximum(m_a, m_b)
    ea = jnp.exp(m_a - m)
    eb = jnp.exp(m_b - m)
    return acc_a * ea + acc_b * eb, m, l_a * ea + l_b * eb


def _attn_kernel(q1_ref, q4_ref, q16_ref,
                 k1_ref, k1p_ref, k4_ref, k4p_ref, k16_ref, k16p_ref,
                 v1_ref, v1p_ref, v4_ref, v4p_ref, v16_ref, v16p_ref,
                 o_ref,
                 a16_ref, m16_ref, l16_ref, a4_ref, m4_ref, l4_ref):
    n = pl.program_id(2)
    d4, d16 = DILATIONS[1], DILATIONS[2]
    sub4 = SUPER // d4
    ratio = d16 // d4
    shape = (BLOCK, HEAD_DIM)

    qi = lax.broadcasted_iota(jnp.int32, (BLOCK, 2 * BLOCK), 0)
    kj = lax.broadcasted_iota(jnp.int32, (BLOCK, 2 * BLOCK), 1)
    band = (kj >= qi) & (kj <= qi + BLOCK)
    bias_band = jnp.where(band, 0.0, NEG).astype(F32)
    bias_head = jnp.where(n == 0, jnp.where(band & (kj >= BLOCK), 0.0, NEG).astype(F32), bias_band)

    def lanes(r):
        return slice(r * HEAD_DIM, (r + 1) * HEAD_DIM)

    for r in range(d16):
        a, bb = r // d4, r % d4
        k = jnp.concatenate([k16p_ref[:, lanes(r)], k16_ref[:, lanes(r)]], axis=0)
        v = jnp.concatenate([v16p_ref[:, lanes(r)], v16_ref[:, lanes(r)]], axis=0)
        acc, m, l = _attn_block(q16_ref[:, lanes(r)], k, v, bias_head)
        rows = pl.ds(bb * sub4 + a, BLOCK, stride=ratio)
        a16_ref[rows, :] = acc
        m16_ref[rows, :] = jnp.broadcast_to(m, shape)
        l16_ref[rows, :] = jnp.broadcast_to(l, shape)

    def branch4(r, jj, k, v, bias):
        q = q4_ref[pl.ds(jj * BLOCK, BLOCK), lanes(r)]
        acc, m, l = _attn_block(q, k, v, bias)
        src = pl.ds(r * sub4 + jj * BLOCK, BLOCK)
        acc, m, l = _merge(acc, m, l, a16_ref[src, :], m16_ref[src, :], l16_ref[src, :])
        rows = pl.ds(jj * (BLOCK * d4) + r, BLOCK, stride=d4)
        a4_ref[rows, :] = acc
        m4_ref[rows, :] = m
        l4_ref[rows, :] = l

    for r in range(d4):
        k = jnp.concatenate([k4p_ref[:, lanes(r)], k4_ref[pl.ds(0, BLOCK), lanes(r)]], axis=0)
        v = jnp.concatenate([v4p_ref[:, lanes(r)], v4_ref[pl.ds(0, BLOCK), lanes(r)]], axis=0)
        branch4(r, 0, k, v, bias_head)
        for jj in range(1, sub4 // BLOCK):
            keys = pl.ds((jj - 1) * BLOCK, 2 * BLOCK)
            branch4(r, jj, k4_ref[keys, lanes(r)], v4_ref[keys, lanes(r)], bias_band)

    def branch1(ii, k, v, bias):
        acc, m, l = _attn_block(q1_ref[pl.ds(ii * BLOCK, BLOCK), :], k, v, bias)
        src = pl.ds(ii * BLOCK, BLOCK)
        acc, m, l = _merge(acc, m, l, a4_ref[src, :], m4_ref[src, :], l4_ref[src, :])
        o_ref[src, :] = (acc / l).astype(o_ref.dtype)

    k = jnp.concatenate([k1p_ref[...], k1_ref[pl.ds(0, BLOCK), :]], axis=0)
    v = jnp.concatenate([v1p_ref[...], v1_ref[pl.ds(0, BLOCK), :]], axis=0)
    branch1(0, k, v, bias_head)
    for ii in range(1, SUPER // BLOCK):
        keys = pl.ds((ii - 1) * BLOCK, 2 * BLOCK)
        branch1(ii, k1_ref[keys, :], v1_ref[keys, :], bias_band)


def _attention(z1, z4, z16, n_attn):
    b, _, s, hd = z1.shape
    d4, d16 = DILATIONS[1], DILATIONS[2]
    views = {1: z1, d4: z4, d16: z16}
    n_super = s // SUPER

    def cur(d, slot0):
        return pl.BlockSpec((None, None, SUPER // d, d * hd),
                            lambda bi, h, n: (bi, slot0 + h, n, 0))

    def prev(d, slot0):
        per = SUPER // d // BLOCK
        return pl.BlockSpec((None, None, BLOCK, d * hd),
                            lambda bi, h, n: (bi, slot0 + h, jnp.maximum(n * per - 1, 0), 0))

    in_specs = [cur(1, 0), cur(d4, 0), cur(d16, 0)]
    args = [views[1], views[d4], views[d16]]
    for slot0 in (n_attn, 2 * n_attn):
        for d in DILATIONS:
            in_specs += [cur(d, slot0), prev(d, slot0)]
            args += [views[d], views[d]]
    scratch = [pltpu.VMEM((SUPER, hd), F32) for _ in range(6)]
    return pl.pallas_call(
        _attn_kernel,
        grid=(b, n_attn, n_super),
        in_specs=in_specs,
        out_specs=pl.BlockSpec((None, SUPER, hd), lambda bi, h, n: (bi, n, h)),
        out_shape=jax.ShapeDtypeStruct((b, s, n_attn * hd), BF16),
        scratch_shapes=scratch,
        compiler_params=_cparams(("arbitrary", "arbitrary", "arbitrary")),
        name="dilated_attention",
    )(*args)


def _sgu_kernel(u_ref, v_ref, w_ref, b_ref, o_ref, *, n_chunks):
    row = lax.broadcasted_iota(jnp.int32, (CHUNK, CHUNK), 0)
    col = lax.broadcasted_iota(jnp.int32, (CHUNK, CHUNK), 1)
    w = (w_ref[...] * (row >= col).astype(F32)).astype(BF16)
    bias = b_ref[...]
    for c in range(n_chunks):
        rows = pl.ds(c * CHUNK, CHUNK)
        v = jax.nn.gelu(v_ref[rows, :].astype(F32))
        u = jax.nn.gelu(u_ref[rows, :].astype(F32))
        mixed = jnp.dot(w, v.astype(BF16), preferred_element_type=F32) + bias
        o_ref[rows, :] = (u * mixed).astype(o_ref.dtype)


def _sgu(zh, w_s, b_s, n_attn, n_sgu, rows):
    b, _, s, hd = zh.shape
    u0, v0 = 3 * n_attn, 3 * n_attn + n_sgu
    kern = functools.partial(_sgu_kernel, n_chunks=rows // CHUNK)
    return pl.pallas_call(
        kern,
        grid=(b, n_sgu, s // rows),
        in_specs=[
            pl.BlockSpec((None, None, rows, hd), lambda bi, g, n: (bi, u0 + g, n, 0)),
            pl.BlockSpec((None, None, rows, hd), lambda bi, g, n: (bi, v0 + g, n, 0)),
            pl.BlockSpec((None, CHUNK, CHUNK), lambda bi, g, n: (g, 0, 0)),
            pl.BlockSpec((None, CHUNK, 1), lambda bi, g, n: (g, 0, 0)),
        ],
        out_specs=pl.BlockSpec((None, rows, hd), lambda bi, g, n: (bi, n, g)),
        out_shape=jax.ShapeDtypeStruct((b, s, n_sgu * hd), BF16),
        compiler_params=_cparams(("arbitrary", "arbitrary", "arbitrary")),
        name="spatial_gating",
    )(zh, zh, w_s, b_s.reshape(n_sgu, CHUNK, 1))


def _outproj_kernel(x_ref, a_ref, s_ref, wa_ref, ws_ref, gate_ref, o_ref):
    acc = jnp.dot(a_ref[...], wa_ref[...], preferred_element_type=F32)
    acc += jnp.dot(s_ref[...], ws_ref[...], preferred_element_type=F32)
    o_ref[...] = x_ref[...] + gate_ref[...] * acc


def _outproj(x, attn, sgu, w_out, li, gate, tm):
    b, s, d = x.shape
    wa, ws = attn.shape[-1], sgu.shape[-1]
    tiles_per_seq = s // tm
    row = lambda i: (i // tiles_per_seq, i % tiles_per_seq, 0)
    return pl.pallas_call(
        _outproj_kernel,
        grid=(b * tiles_per_seq,),
        in_specs=[
            pl.BlockSpec((None, tm, d), row),
            pl.BlockSpec((None, tm, wa), row),
            pl.BlockSpec((None, tm, ws), row),
            pl.BlockSpec((None, wa, d), lambda i: (li, 0, 0)),
            pl.BlockSpec((None, ws, d), lambda i: (li, wa // ws, 0)),
            pl.BlockSpec((None, 1, d), lambda i: (i // tiles_per_seq, 0, 0)),
        ],
        out_specs=pl.BlockSpec((None, tm, d), row),
        out_shape=jax.ShapeDtypeStruct((b, s, d), F32),
        compiler_params=_cparams(("arbitrary",)),
        name="even_outproj",
    )(x, attn, sgu, w_out, w_out, gate)


def _conv_kernel(x_hbm, g_ref, sc_ref, sh_ref, scn_ref, shn_ref, gate_ref, wb_ref, wc_ref, wx_ref,
                 cw_ref, wo_ref, o_ref, xbuf, h_ref, hn_ref, sem, *, tm, tiles_per_seq, n_tiles,
                 chunks_per_step):
    i, j = pl.program_id(0), pl.program_id(1)
    has_next = i + 1 < n_tiles

    def x_copy(tile):
        rows = pl.ds(pl.multiple_of((tile % tiles_per_seq) * tm, tm), tm)
        return pltpu.make_async_copy(x_hbm.at[tile // tiles_per_seq, rows, :], xbuf, sem)

    def step(one=None):
        h_ext = h_ref[...]
        y = (jnp.dot(h_ext, wc_ref[...], preferred_element_type=F32)
             * jnp.dot(h_ext, wx_ref[...], preferred_element_type=F32))
        cw = cw_ref[...]
        conv = cw[2:3] * y + cw[1:2] * pltpu.roll(y, 1, 0) + cw[0:1] * pltpu.roll(y, 2, 0)
        gb = jnp.dot(h_ref[pl.ds(HALO, tm), :], wb_ref[...], preferred_element_type=F32)
        if one is not None:
            gb = gb * one
        a = (gb * conv[HALO:]).astype(BF16)
        o_ref[...] += gate_ref[...] * jnp.dot(a, wo_ref[...], preferred_element_type=F32)

    @pl.when(j == 0)
    def _():
        @pl.when(i == 0)
        def _():
            first = x_copy(0)
            first.start()
            first.wait()
            _normmod_rows(h_ref, xbuf, g_ref, sc_ref, sh_ref, tm, offset=HALO)

        @pl.when(i > 0)
        def _():
            h_ref[pl.ds(0, HALO), :] = h_ref[pl.ds(tm, HALO), :]
            h_ref[pl.ds(HALO, tm), :] = hn_ref[...]

        @pl.when(i % tiles_per_seq == 0)
        def _():
            h_ref[pl.ds(0, HALO), :] = jnp.zeros((HALO, h_ref.shape[1]), BF16)

        o_ref[...] = xbuf[...]

        @pl.when(has_next)
        def _():
            x_copy(i + 1).start()

    @pl.when(j < NORM_FROM)
    def _():
        step()

        @pl.when((j == NORM_FROM - 1) & has_next)
        def _():
            x_copy(i + 1).wait()

    @pl.when(j >= NORM_FROM)
    def _():
        step(_norm_next_rows(hn_ref, xbuf, g_ref, scn_ref, shn_ref, j - NORM_FROM, chunks_per_step, tm))


def _conv_mixer(x, g, sc, sh, gate, w_in, conv_w, w_out, li, tm, tc):
    b, s, d = x.shape
    tiles_per_seq = s // tm
    n_tiles = b * tiles_per_seq
    nt = d // tc
    norm_steps = nt - NORM_FROM
    assert norm_steps >= 1
    chunks_per_step = -(-(tm // NORM_ROWS) // norm_steps)
    row = lambda i, j: (i // tiles_per_seq, i % tiles_per_seq, 0)
    per_b = lambda i, j: (i // tiles_per_seq, 0, 0)
    nxt = lambda i: jnp.minimum(i + 1, n_tiles - 1)
    per_b_next = lambda i, j: (nxt(i) // tiles_per_seq, 0, 0)
    kern = functools.partial(_conv_kernel, tm=tm, tiles_per_seq=tiles_per_seq, n_tiles=n_tiles,
                             chunks_per_step=chunks_per_step)
    return pl.pallas_call(
        kern,
        grid=(n_tiles, nt),
        in_specs=[
            pl.BlockSpec(memory_space=pl.ANY),
            pl.BlockSpec((1, d), lambda i, j: (0, 0)),
            pl.BlockSpec((None, 1, d), per_b),
            pl.BlockSpec((None, 1, d), per_b),
            pl.BlockSpec((None, 1, d), per_b_next),
            pl.BlockSpec((None, 1, d), per_b_next),
            pl.BlockSpec((None, 1, d), per_b),
            pl.BlockSpec((None, d, tc), lambda i, j: (li, 0, j)),
            pl.BlockSpec((None, d, tc), lambda i, j: (li, 0, nt + j)),
            pl.BlockSpec((None, d, tc), lambda i, j: (li, 0, 2 * nt + j)),
            pl.BlockSpec((CONV_WIDTH, tc), lambda i, j: (0, j)),
            pl.BlockSpec((None, tc, d), lambda i, j: (li, j, 0)),
        ],
        out_specs=pl.BlockSpec((None, tm, d), row),
        out_shape=jax.ShapeDtypeStruct((b, s, d), F32),
        scratch_shapes=[
            pltpu.VMEM((tm, d), F32),
            pltpu.VMEM((HALO + tm, d), BF16),
            pltpu.VMEM((tm, d), BF16),
            pltpu.SemaphoreType.DMA(()),
        ],
        compiler_params=_cparams(("arbitrary", "arbitrary")),
        name="conv_mixer",
    )(x, g, sc, sh, sc, sh, gate, w_in, w_in, w_in, conv_w, w_out)


def _ffn_kernel(x_hbm, g_ref, sc_ref, sh_ref, scn_ref, shn_ref, gate_ref, wg_ref, wu_ref, wd_ref,
                fn_ref, o_ref, xbuf, h_ref, hn_ref, sem, *, final, n_f, tm, tiles_per_seq, n_tiles,
                chunks_per_step):
    i, f = pl.program_id(0), pl.program_id(1)
    has_next = i + 1 < n_tiles

    def x_copy(tile):
        rows = pl.ds(pl.multiple_of((tile % tiles_per_seq) * tm, tm), tm)
        return pltpu.make_async_copy(x_hbm.at[tile // tiles_per_seq, rows, :], xbuf, sem)

    def step(one=None):
        h = h_ref[...]
        gt = jnp.dot(h, wg_ref[...], preferred_element_type=F32)
        up = jnp.dot(h, wu_ref[...], preferred_element_type=F32)
        if one is not None:
            up = up * one
        a = (jax.nn.silu(gt) * up).astype(BF16)
        o_ref[...] += gate_ref[...] * jnp.dot(a, wd_ref[...], preferred_element_type=F32)

    @pl.when(f == 0)
    def _():
        @pl.when(i == 0)
        def _():
            first = x_copy(0)
            first.start()
            first.wait()
            _normmod_rows(h_ref, xbuf, g_ref, sc_ref, sh_ref, tm)

        @pl.when(i > 0)
        def _():
            h_ref[...] = hn_ref[...]

        o_ref[...] = xbuf[...]

        @pl.when(has_next)
        def _():
            x_copy(i + 1).start()

    @pl.when(f < NORM_FROM)
    def _():
        step()

        @pl.when((f == NORM_FROM - 1) & has_next)
        def _():
            x_copy(i + 1).wait()

    @pl.when(f >= NORM_FROM)
    def _():
        step(_norm_next_rows(hn_ref, xbuf, g_ref, scn_ref, shn_ref, f - NORM_FROM, chunks_per_step, tm))

    if final:
        @pl.when(f == n_f - 1)
        def _():
            y = o_ref[...]
            o_ref[...] = (y * lax.rsqrt(jnp.mean(y * y, axis=-1, keepdims=True) + EPS)) * fn_ref[...]


def _ffn(x, g, sc, sh, gate, w_gate, w_up, w_down, li, final_gain, final, tm, tf):
    b, s, d = x.shape
    dff = w_gate.shape[-1]
    n_f = dff // tf
    tiles_per_seq = s // tm
    n_tiles = b * tiles_per_seq
    n_chunks = tm // NORM_ROWS
    norm_steps = n_f - NORM_FROM
    assert norm_steps >= 1
    chunks_per_step = -(-n_chunks // norm_steps)
    row = lambda i, f: (i // tiles_per_seq, i % tiles_per_seq, 0)
    per_b = lambda i, f: (i // tiles_per_seq, 0, 0)
    nxt = lambda i: jnp.minimum(i + 1, n_tiles - 1)
    per_b_next = lambda i, f: (nxt(i) // tiles_per_seq, 0, 0)
    const = lambda i, f: (0, 0)
    kern = functools.partial(_ffn_kernel, final=final, n_f=n_f, tm=tm, tiles_per_seq=tiles_per_seq,
                             n_tiles=n_tiles, chunks_per_step=chunks_per_step)
    return pl.pallas_call(
        kern,
        grid=(n_tiles, n_f),
        in_specs=[
            pl.BlockSpec(memory_space=pl.ANY),
            pl.BlockSpec((1, d), const),
            pl.BlockSpec((None, 1, d), per_b),
            pl.BlockSpec((None, 1, d), per_b),
            pl.BlockSpec((None, 1, d), per_b_next),
            pl.BlockSpec((None, 1, d), per_b_next),
            pl.BlockSpec((None, 1, d), per_b),
            pl.BlockSpec((None, d, tf), lambda i, f: (li, 0, f)),
            pl.BlockSpec((None, d, tf), lambda i, f: (li, 0, f)),
            pl.BlockSpec((None, tf, d), lambda i, f: (li, f, 0)),
            pl.BlockSpec((1, d), const),
        ],
        out_specs=pl.BlockSpec((None, tm, d), row),
        out_shape=jax.ShapeDtypeStruct((b, s, d), F32),
        scratch_shapes=[
            pltpu.VMEM((tm, d), F32),
            pltpu.VMEM((tm, d), BF16),
            pltpu.VMEM((tm, d), BF16),
            pltpu.SemaphoreType.DMA(()),
        ],
        compiler_params=_cparams(("arbitrary", "arbitrary")),
        name="swiglu_ffn_final" if final else "swiglu_ffn",
    )(x, g, sc, sh, sc, sh, gate, w_gate, w_up, w_down, final_gain)


def _rope_tables(positions):
    inv_freq = ROPE_THETA ** (-jnp.arange(0, ROPE_DIM, 2, dtype=F32) / ROPE_DIM)
    ang = positions.astype(F32)[..., None] * inv_freq
    cos, sin = jnp.cos(ang), jnp.sin(ang)
    gap_one = jnp.ones(ang.shape[:-1] + (HEAD_DIM // 2 - ROPE_HALF,), F32)
    gap_zero = jnp.zeros_like(gap_one)
    cosf = jnp.concatenate([cos, gap_one, cos, gap_one], axis=-1)
    sinf = jnp.concatenate([-sin, gap_zero, sin, gap_zero], axis=-1)
    return cosf, sinf


def _rope_head_order():
    fill = HEAD_DIM // 2 - ROPE_HALF
    return (list(range(0, ROPE_HALF)) + list(range(ROPE_DIM, ROPE_DIM + fill))
            + list(range(ROPE_HALF, ROPE_DIM)) + list(range(ROPE_DIM + fill, HEAD_DIM)))


def _permute_qk_columns(w_in, n_attn):
    lead = w_in.shape[:-1]
    n_qk = 2 * n_attn * HEAD_DIM
    order = jnp.asarray(_rope_head_order(), jnp.int32)
    select = (jnp.arange(HEAD_DIM, dtype=jnp.int32)[:, None] == order[None, :]).astype(w_in.dtype)
    qk = w_in[..., :n_qk].reshape(lead + (2 * n_attn, HEAD_DIM))
    qk = jnp.einsum("...hc,ce->...he", qk, select, preferred_element_type=F32).astype(w_in.dtype)
    return jnp.concatenate([qk.reshape(lead + (n_qk,)), w_in[..., n_qk:]], axis=-1)


def _pick(n, candidates):
    for c in candidates:
        if n % c == 0:
            return c
    return n


def kernel(x, c, positions, ada_w, ada_b, norm_mix, norm_ffn, ab_w_in, sgu_w, sgu_b, ab_w_out,
           conv_w_in, conv_w, conv_w_out, ffn_w_gate, ffn_w_up, ffn_w_down, final_norm):
    b, s, d = x.shape
    depth = ada_w.shape[0]
    n_slots = d // HEAD_DIM
    n_attn = 3 * n_slots // 4
    n_sgu = n_slots - n_attn
    dff = ffn_w_gate.shape[-1]
    assert s % SUPER == 0 and d % HEAD_DIM == 0

    tm_big = _pick(s, (1024, 512, 256, 128))
    tm_small = _pick(s, (512, 256, 128))
    tf = _pick(dff, (512, 256, 128))
    tc = _pick(d // 4, (512, 256, 128))

    mod = _modulation(c, ada_w, ada_b)
    mod = mod.reshape(depth, b, 6, 1, d)
    rope = _rope_tables(positions)
    final_gain = final_norm.reshape(1, d)
    ab_w_in, ab_w_out, conv_w_in, conv_w_out, ffn_w_gate, ffn_w_up, ffn_w_down = (
        w.astype(BF16) for w in (ab_w_in, ab_w_out, conv_w_in, conv_w_out, ffn_w_gate, ffn_w_up,
                                 ffn_w_down))
    ab_w_in = _permute_qk_columns(ab_w_in, n_attn)

    for layer in range(depth):
        sh_m, sc_m, g_m, sh_f, sc_f, g_f = (mod[layer, :, k] for k in range(6))
        gain_m = norm_mix[layer].reshape(1, d)
        gain_f = norm_ffn[layer].reshape(1, d)
        i = layer // 2
        if layer % 2 == 0:
            zh, z4, z16 = _inproj(x, gain_m, sc_m, sh_m, ab_w_in, i, rope, n_attn, n_sgu, tm_big)
            attn = _attention(zh, z4, z16, n_attn)
            sgu = _sgu(zh, sgu_w[i], sgu_b[i], n_attn, n_sgu, _pick(s, (2048, 1024, 512, 256, 128)))
            x = _outproj(x, attn, sgu, ab_w_out, i, g_m, tm_small)
        else:
            x = _conv_mixer(x, gain_m, sc_m, sh_m, g_m, conv_w_in, conv_w[i], conv_w_out, i,
                            tm_big, tc)
        x = _ffn(x, gain_f, sc_f, sh_f, g_f, ffn_w_gate, ffn_w_up, ffn_w_down, layer, final_gain,
                 layer == depth - 1, tm_big, tf)
    return x
```
